```python
import math
import jax, jax.numpy as jnp
from jax import lax
import numpy as np

D_MODEL = 1024
BATCH = 8
SEQ = 4096
DEPTH = 2

HG_HEAD_DIM = 128
HG_HEADS = D_MODEL // HG_HEAD_DIM
HG_DIM = HG_HEADS * HG_HEAD_DIM
GLA_HEADS = 4
GLA_KEY_DIM = D_MODEL // 2
GLA_VALUE_DIM = D_MODEL
GLA_HEAD_K = GLA_KEY_DIM // GLA_HEADS
GLA_HEAD_V = GLA_VALUE_DIM // GLA_HEADS
GLA_GATE_RANK = 16
GLA_GATE_NORMALIZER = 16.0
GLA_IN_DIM = 2 * GLA_KEY_DIM + 2 * GLA_VALUE_DIM + GLA_GATE_RANK
D_FF = 2816
CONV_WIDTH = 3
CHUNK = 64
EPS = 1e-6
N_HG_LAYERS = (DEPTH + 1) // 2
N_GLA_LAYERS = DEPTH // 2

kernel_name = "hybrid_hgrn2_gla_convglu"


def rmsnorm(x, w):
    xf = x.astype(jnp.float32)
    xf = xf * lax.rsqrt(jnp.mean(xf * xf, axis=-1, keepdims=True) + EPS)
    return (xf * w.astype(jnp.float32)).astype(x.dtype)


def chunk_gated_linear_attention(q, k, v, log_f):
    B, T, H, dk = q.shape
    dv = v.shape[-1]
    N = T // CHUNK

    def to_chunks(a):
        return a.astype(jnp.float32).reshape(B, N, CHUNK, H, a.shape[-1]).transpose(1, 0, 3, 2, 4)

    qc = to_chunks(q) * (dk ** -0.5)
    kc, vc, gc = to_chunks(k), to_chunks(v), to_chunks(log_f)
    causal = jnp.tril(jnp.ones((CHUNK, CHUNK), dtype=bool))

    def step(S, chunk):
        q_, k_, v_, g_ = chunk
        b = jnp.cumsum(g_, axis=2)
        diff = b[:, :, :, None, :] - b[:, :, None, :, :]
        decay = jnp.exp(jnp.where(causal[:, :, None], diff, -jnp.inf))
        scores = jnp.einsum('bhtd,bhtsd,bhsd->bhts', q_, decay, k_)
        o = (jnp.einsum('bhts,bhsv->bhtv', scores, v_)
             + jnp.einsum('bhtd,bhdv->bhtv', q_ * jnp.exp(b), S))
        b_last = b[:, :, -1:, :]
        S = (jnp.exp(b_last[:, :, 0, :])[..., None] * S
             + jnp.einsum('bhsd,bhsv->bhdv', k_ * jnp.exp(b_last - b), v_))
        return S, o

    S0 = jnp.zeros((B, H, dk, dv), jnp.float32)
    _, o = lax.scan(step, S0, (qc, kc, vc, gc))
    return o.transpose(1, 0, 3, 2, 4).reshape(B, T, H, dv)


def gated_head_norm(o, g, norm_w, out_dtype):
    B, T, H, dv = o.shape
    o = o * lax.rsqrt(jnp.mean(o * o, axis=-1, keepdims=True) + EPS) * norm_w.astype(jnp.float32)
    o = o.reshape(B, T, H * dv) * jax.nn.silu(g.astype(jnp.float32))
    return o.astype(out_dtype)


def hgrn2_mixer(h, w_in, lb, norm_w, w_out):
    B, T, _ = h.shape
    proj = h @ w_in
    q, f, i, g = jnp.split(proj, 4, axis=-1)
    q = jax.nn.silu(q)
    forget = lb + (1.0 - lb) * jax.nn.sigmoid(f.astype(jnp.float32))
    k = 1.0 - forget
    log_f = jnp.log(forget)
    heads = lambda a: a.reshape(B, T, HG_HEADS, HG_HEAD_DIM)
    o = chunk_gated_linear_attention(heads(q), heads(k), heads(i), heads(log_f))
    return gated_head_norm(o, g, norm_w, h.dtype) @ w_out


def gla_mixer(h, w_in, w_gk_up, b_gk_up, norm_w, w_out):
    B, T, _ = h.shape
    proj = h @ w_in
    splits = np.cumsum([GLA_KEY_DIM, GLA_KEY_DIM, GLA_VALUE_DIM, GLA_VALUE_DIM]).tolist()
    q, k, v, g, gate_lr = jnp.split(proj, splits, axis=-1)
    gk = (gate_lr @ w_gk_up + b_gk_up).astype(jnp.float32)
    log_f = jax.nn.log_sigmoid(gk) / GLA_GATE_NORMALIZER
    hk = lambda a: a.reshape(B, T, GLA_HEADS, GLA_HEAD_K)
    o = chunk_gated_linear_attention(hk(q), hk(k), v.reshape(B, T, GLA_HEADS, GLA_HEAD_V), hk(log_f))
    return gated_head_norm(o, g, norm_w, h.dtype) @ w_out


def conv_glu_ffn(h, w_up, conv_w, conv_b, w_down):
    T = h.shape[1]
    u = h @ w_up
    up = jnp.pad(u, ((0, 0), (CONV_WIDTH - 1, 0), (0, 0)))
    u = conv_b + sum(conv_w[j] * up[:, j:j + T] for j in range(CONV_WIDTH))
    a, gate = jnp.split(u, 2, axis=-1)
    return (jax.nn.silu(gate) * a) @ w_down


def setup_inputs(seed: int = 0) -> dict:
    key = jax.random.key(seed)
    ks = jax.random.split(key, 20)
    nrm = lambda k, shape, fan_in: jax.random.normal(k, shape, jnp.float32) * (fan_in ** -0.5)
    gain = lambda k, shape: 1.0 + 0.02 * jax.random.normal(k, shape, jnp.float32)
    return {
        "x": jax.random.normal(ks[0], (BATCH, SEQ, D_MODEL), jnp.float32),
        "lb_logits": 0.1 * jax.random.normal(ks[1], (DEPTH + 1, HG_DIM), jnp.float32),
        "hg_w_in": nrm(ks[2], (N_HG_LAYERS, D_MODEL, 4 * HG_DIM), D_MODEL),
        "hg_norm_w": gain(ks[3], (N_HG_LAYERS, HG_HEAD_DIM)),
        "hg_w_out": nrm(ks[4], (N_HG_LAYERS, HG_DIM, D_MODEL), HG_DIM),
        "gla_w_in": nrm(ks[5], (N_GLA_LAYERS, D_MODEL, GLA_IN_DIM), D_MODEL),
        "gla_w_gk_up": nrm(ks[6], (N_GLA_LAYERS, GLA_GATE_RANK, GLA_KEY_DIM), GLA_GATE_RANK),
        "gla_b_gk_up": 0.1 * jax.random.normal(ks[7], (N_GLA_LAYERS, GLA_KEY_DIM), jnp.float32),
        "gla_norm_w": gain(ks[8], (N_GLA_LAYERS, GLA_HEAD_V)),
        "gla_w_out": nrm(ks[9], (N_GLA_LAYERS, GLA_VALUE_DIM, D_MODEL), GLA_VALUE_DIM),
        "norm_mixer_w": gain(ks[10], (DEPTH, D_MODEL)),
        "norm_ffn_w": gain(ks[11], (DEPTH, D_MODEL)),
        "ffn_w_up": nrm(ks[12], (DEPTH, D_MODEL, 2 * D_FF), D_MODEL),
        "ffn_conv_w": nrm(ks[13], (DEPTH, CONV_WIDTH, 2 * D_FF), CONV_WIDTH),
        "ffn_conv_b": 0.02 * jax.random.normal(ks[14], (DEPTH, 2 * D_FF), jnp.float32),
        "ffn_w_down": nrm(ks[15], (DEPTH, D_FF, D_MODEL), D_FF),
        "norm_final_w": gain(ks[16], (D_MODEL,)),
    }


def reference(x, lb_logits, hg_w_in, hg_norm_w, hg_w_out, gla_w_in, gla_w_gk_up, gla_b_gk_up,
              gla_norm_w, gla_w_out, norm_mixer_w, norm_ffn_w, ffn_w_up, ffn_conv_w, ffn_conv_b,
              ffn_w_down, norm_final_w):
    lb_table = jnp.cumsum(jax.nn.softmax(lb_logits.astype(jnp.float32), axis=0), axis=0)
    h = x
    for i in range(DEPTH):
        y = rmsnorm(h, norm_mixer_w[i])
        j = i // 2
        if i % 2 == 0:
            mix = hgrn2_mixer(y, hg_w_in[j], lb_table[i], hg_norm_w[j], hg_w_out[j])
        else:
            mix = gla_mixer(y, gla_w_in[j], gla_w_gk_up[j], gla_b_gk_up[j], gla_norm_w[j], gla_w_out[j])
        h = h + mix
        y = rmsnorm(h, norm_ffn_w[i])
        h = h + conv_glu_ffn(y, ffn_w_up[i], ffn_conv_w[i], ffn_conv_b[i], ffn_w_down[i])
    return rmsnorm(h, norm_final_w)
```

```python
import functools
import math

import jax
import jax.numpy as jnp
from jax import lax
from jax.experimental import pallas as pl
from jax.experimental.pallas import tpu as pltpu

EPS = 1e-6
GLA_GATE_NORMALIZER = 16.0
LANES = 128
VMEM_LIMIT = 56 * 1024 * 1024

F32 = jnp.float32
BF16 = jnp.bfloat16


def _rms(x, w):
    return x * lax.rsqrt(jnp.mean(x * x, axis=-1, keepdims=True) + EPS) * w


def _sigmoid(x):
    return 1.0 / (1.0 + jnp.exp(-x))


def _dot(a, b):
    return jnp.dot(a, b, preferred_element_type=F32)


def _dot_nt(a, b):
    return lax.dot_general(a, b, (((1,), (1,)), ((), ())), preferred_element_type=F32)


def _dot_tn(a, b):
    return lax.dot_general(a, b, (((0,), (0,)), ((), ())), preferred_element_type=F32)


def _store_heads(ref, val, heads):
    d = val.shape[-1] // heads
    for h in range(heads):
        ref[h] = val[:, h * d:(h + 1) * d].astype(ref.dtype)


def _hg_proj_kernel(slot, heads, x_ref, nw_ref, lbl_ref, w_ref, q_ref, k_ref, v_ref, lf_ref, g_ref):
    y = _rms(x_ref[...], nw_ref[...]).astype(BF16)
    dim = g_ref.shape[-1]
    dk = dim // heads
    sect = lambda s: _dot(y, w_ref[:, s * dim:(s + 1) * dim])
    l = lbl_ref[...]
    e = jnp.exp(l - jnp.max(l, axis=0, keepdims=True))
    lb = jnp.sum(e[:slot + 1], axis=0, keepdims=True) / jnp.sum(e, axis=0, keepdims=True)
    q = sect(0)
    _store_heads(q_ref, q * _sigmoid(q) * (dk ** -0.5), heads)
    forget = lb + (1.0 - lb) * _sigmoid(sect(1))
    _store_heads(k_ref, 1.0 - forget, heads)
    _store_heads(lf_ref, jnp.log(forget), heads)
    _store_heads(v_ref, sect(2), heads)
    g_ref[...] = sect(3)


def _gla_proj_kernel(heads, kd, vd, x_ref, nw_ref, w_ref, wlr_ref, wup_ref, bup_ref,
                     q_ref, k_ref, v_ref, lf_ref, g_ref):
    y = _rms(x_ref[...], nw_ref[...]).astype(BF16)
    dk = kd // heads
    _store_heads(q_ref, _dot(y, w_ref[:, 0:kd]) * (dk ** -0.5), heads)
    _store_heads(k_ref, _dot(y, w_ref[:, kd:2 * kd]), heads)
    _store_heads(v_ref, _dot(y, w_ref[:, 2 * kd:2 * kd + vd]), heads)
    g_ref[...] = _dot(y, w_ref[:, 2 * kd + vd:2 * kd + 2 * vd])
    lr = _dot(y, wlr_ref[...]).astype(BF16)
    gk = _dot(lr, wup_ref[...]) + bup_ref[...]
    log_sig = jnp.minimum(gk, 0.0) - jnp.log(1.0 + jnp.exp(-jnp.abs(gk)))
    _store_heads(lf_ref, log_sig / GLA_GATE_NORMALIZER, heads)


def _level_ref_rows(c, m):
    C, d = c.shape
    if m >= 8:
        blk = c.reshape(C // (2 * m), 2 * m, d)
        return jnp.broadcast_to(blk[:, m - 1:m, :], blk.shape).reshape(C, d)
    x3 = c.reshape(C // 8, 8, d)
    pick = lambda r: jnp.broadcast_to(x3[:, r:r + 1, :], x3.shape)
    sub = lax.broadcasted_iota(jnp.int32, x3.shape, 1)
    if m == 4:
        out = pick(3)
    elif m == 2:
        out = jnp.where(sub < 4, pick(1), pick(5))
    else:
        out = jnp.where(sub < 2, pick(0), jnp.where(sub < 4, pick(2), jnp.where(sub < 6, pick(4), pick(6))))
    return out.reshape(C, d)


def _attn_kernel(heads, q_ref, k_ref, v_ref, lf_ref, o_ref, st_ref):
    @pl.when(pl.program_id(1) == 0)
    def _():
        st_ref[...] = jnp.zeros_like(st_ref)

    C, dk = q_ref.shape[1], q_ref.shape[2]
    rows = lax.broadcasted_iota(jnp.int32, (C, dk), 0)
    ti = lax.broadcasted_iota(jnp.int32, (C, C), 0)
    si = lax.broadcasted_iota(jnp.int32, (C, C), 1)
    split_bit = jnp.where(ti > si, ti ^ si, 0)

    def head(h, carry):
        q = q_ref[h]
        k = k_ref[h]
        v = v_ref[h].astype(BF16)
        c = lf_ref[h]
        a = jnp.where(ti == si, _dot_nt(q.astype(BF16), k.astype(BF16)), 0.0)
        m = 1
        while m < C:
            upper = (rows & m) != 0
            r = _level_ref_rows(c, m)
            f = jnp.exp(jnp.where(upper, c, r - c))
            p = _dot_nt((q * f).astype(BF16), (k * f).astype(BF16))
            a = jnp.where((split_bit >= m) & (split_bit < 2 * m), p, a)
            c = jnp.where(upper, c + r, c)
            m *= 2
        st = st_ref[h]
        o = _dot(a.astype(BF16), v) + _dot_nt((q * jnp.exp(c)).astype(BF16), st.astype(BF16))
        o_ref[h] = o
        c_last = c[C - 1:C, :]
        kd = (k * jnp.exp(c_last - c)).astype(BF16)
        st_ref[h] = st * jnp.exp(c_last) + _dot_tn(v, kd)
        return carry

    lax.fori_loop(0, heads, head, 0)


def _dense_kernel(heads, final, o_ref, g_ref, hnw_ref, wout_ref, h_ref, fnw_ref, wup_ref, cw_ref,
                  cb_ref, wdown_ref, finw_ref, out_ref, carry_ref):
    tm = h_ref.shape[0]
    dff = wdown_ref.shape[0]

    @pl.when(pl.program_id(1) == 0)
    def _():
        carry_ref[...] = jnp.zeros_like(carry_ref)

    hw = hnw_ref[...]
    parts = []
    for hh in range(heads):
        o = o_ref[hh]
        parts.append(o * lax.rsqrt(jnp.mean(o * o, axis=-1, keepdims=True) + EPS) * hw)
    g = g_ref[...]
    mix_in = (jnp.concatenate(parts, axis=-1) * (g * _sigmoid(g))).astype(BF16)
    h = h_ref[...] + _dot(mix_in, wout_ref[...])

    y = _rms(h, fnw_ref[...]).astype(BF16)
    slab = 2 * LANES
    acc = h
    for n in range(dff // slab):
        cols = []
        for base in (n * slab, dff + n * slab):
            u = _dot(y, wup_ref[:, base:base + slab])
            ext = jnp.concatenate([carry_ref[:, base:base + slab], u], axis=0)
            carry_ref[:, base:base + slab] = u[tm - 8:tm]
            cw = cw_ref[:, base:base + slab]
            cols.append(cb_ref[:, base:base + slab] + cw[0:1] * ext[6:6 + tm] + cw[1:2] * ext[7:7 + tm]
                        + cw[2:3] * u)
        a, gate = cols
        s = (gate * _sigmoid(gate) * a).astype(BF16)
        acc = acc + _dot(s, wdown_ref[n * slab:(n + 1) * slab, :])
    if final:
        acc = _rms(acc, finw_ref[...])
    out_ref[...] = acc


def _const_spec(shape):
    nd = len(shape)
    return pl.BlockSpec(shape, lambda *_: (0,) * nd, pipeline_mode=pl.Buffered(1))


def _params(sem):
    return pltpu.CompilerParams(dimension_semantics=sem, vmem_limit_bytes=VMEM_LIMIT)


def _head_out(B, T, heads, d, tm):
    return (jax.ShapeDtypeStruct((B, heads, T, d), F32),
            pl.BlockSpec((None, heads, tm, d), lambda b, i: (b, 0, i, 0)))


def _hg_proj(h, nw, lb_logits, w_in, slot, heads, tm):
    B, T, D = h.shape
    dim = w_in.shape[1] // 4
    dk = dim // heads
    hs, hspec = _head_out(B, T, heads, dk, tm)
    tok_spec = pl.BlockSpec((None, tm, dim), lambda b, i: (b, i, 0))
    return pl.pallas_call(
        functools.partial(_hg_proj_kernel, slot, heads),
        grid=(B, T // tm),
        in_specs=[pl.BlockSpec((None, tm, D), lambda b, i: (b, i, 0)), _const_spec((1, D)),
                  _const_spec(lb_logits.shape), _const_spec(w_in.shape)],
        out_specs=[hspec, hspec, hspec, hspec, tok_spec],
        out_shape=[hs, hs, hs, hs, jax.ShapeDtypeStruct((B, T, dim), F32)],
        compiler_params=_params(("parallel", "parallel")),
        name="hg_proj",
    )(h, nw.reshape(1, D), lb_logits, w_in)


def _gla_proj(h, nw, w_main, w_lr, w_up, b_up, heads, kd, vd, tm):
    B, T, D = h.shape
    ks, kspec = _head_out(B, T, heads, kd // heads, tm)
    vs, vspec = _head_out(B, T, heads, vd // heads, tm)
    tok_spec = pl.BlockSpec((None, tm, vd), lambda b, i: (b, i, 0))
    return pl.pallas_call(
        functools.partial(_gla_proj_kernel, heads, kd, vd),
        grid=(B, T // tm),
        in_specs=[pl.BlockSpec((None, tm, D), lambda b, i: (b, i, 0)), _const_spec((1, D)),
                  _const_spec(w_main.shape), _const_spec(w_lr.shape), _const_spec(w_up.shape),
                  _const_spec((1, kd))],
        out_specs=[kspec, kspec, vspec, kspec, tok_spec],
        out_shape=[ks, ks, vs, ks, jax.ShapeDtypeStruct((B, T, vd), F32)],
        compiler_params=_params(("parallel", "parallel")),
        name="gla_proj",
    )(h, nw.reshape(1, D), w_main, w_lr, w_up, b_up.reshape(1, kd))


def _attention(q, k, v, lf, chunk):
    B, H, T, dk = q.shape
    dv = v.shape[-1]
    kspec = pl.BlockSpec((None, H, chunk, dk), lambda b, c: (b, 0, c, 0))
    vspec = pl.BlockSpec((None, H, chunk, dv), lambda b, c: (b, 0, c, 0))
    return pl.pallas_call(
        functools.partial(_attn_kernel, H),
        grid=(B, T // chunk),
        in_specs=[kspec, kspec, vspec, kspec],
        out_specs=vspec,
        out_shape=jax.ShapeDtypeStruct((B, H, T, dv), F32),
        scratch_shapes=[pltpu.VMEM((H, dv, dk), F32)],
        compiler_params=_params(("parallel", "arbitrary")),
        name="chunk_attn",
    )(q, k, v, lf)


def _dense(o, g, hnw, w_out, h, fnw, w_up, conv_w, conv_b, w_down, finw, final, tm):
    B, H, T, dv = o.shape
    D = h.shape[-1]
    dff = w_down.shape[0]
    tok = lambda d: pl.BlockSpec((None, tm, d), lambda b, i: (b, i, 0))
    return pl.pallas_call(
        functools.partial(_dense_kernel, H, final),
        grid=(B, T // tm),
        in_specs=[pl.BlockSpec((None, H, tm, dv), lambda b, i: (b, 0, i, 0)), tok(H * dv),
                  _const_spec((1, dv)), _const_spec(w_out.shape), tok(D), _const_spec((1, D)),
                  _const_spec(w_up.shape), _const_spec(conv_w.shape), _const_spec((1, 2 * dff)),
                  _const_spec(w_down.shape), _const_spec((1, D))],
        out_specs=tok(D),
        out_shape=jax.ShapeDtypeStruct((B, T, D), F32),
        scratch_shapes=[pltpu.VMEM((8, 2 * dff), F32)],
        compiler_params=_params(("parallel", "arbitrary")),
        name="dense_tail",
    )(o, g, hnw.reshape(1, dv), w_out, h, fnw.reshape(1, D), w_up, conv_w, conv_b.reshape(1, 2 * dff),
      w_down, finw.reshape(1, D))


def kernel(x, lb_logits, hg_w_in, hg_norm_w, hg_w_out, gla_w_in, gla_w_gk_up, gla_b_gk_up, gla_norm_w,
           gla_w_out, norm_mixer_w, norm_ffn_w, ffn_w_up, ffn_conv_w, ffn_conv_b, ffn_w_down, norm_final_w):
    depth = norm_mixer_w.shape[0]
    hg_heads = hg_w_out.shape[1] // hg_norm_w.shape[-1]
    gla_vd = gla_w_out.shape[1]
    gla_heads = gla_vd // gla_norm_w.shape[-1]
    gla_kd = gla_w_gk_up.shape[-1]
    rank = gla_w_gk_up.shape[1]
    chunk = 128
    tm_proj = 512
    tm_dense = 256
    bf = lambda w: w.astype(BF16)

    h = x
    for i in range(depth):
        j = i // 2
        if i % 2 == 0:
            q, k, v, lf, g = _hg_proj(h, norm_mixer_w[i], lb_logits, bf(hg_w_in[j]), i, hg_heads, tm_proj)
            hnw, w_out = hg_norm_w[j], hg_w_out[j]
        else:
            n_main = 2 * gla_kd + 2 * gla_vd
            w_in = gla_w_in[j]
            w_lr = jnp.pad(w_in[:, n_main:], ((0, 0), (0, LANES - rank)))
            w_up = jnp.pad(gla_w_gk_up[j], ((0, LANES - rank), (0, 0)))
            q, k, v, lf, g = _gla_proj(h, norm_mixer_w[i], bf(w_in[:, :n_main]), bf(w_lr), bf(w_up),
                                       gla_b_gk_up[j], gla_heads, gla_kd, gla_vd, tm_proj)
            hnw, w_out = gla_norm_w[j], gla_w_out[j]
        o = _attention(q, k, v, lf, chunk)
        h = _dense(o, g, hnw, bf(w_out), h, norm_ffn_w[i], bf(ffn_w_up[i]), ffn_conv_w[i], ffn_conv_b[i],
                   bf(ffn_w_down[i]), norm_final_w, i == depth - 1, tm_dense)
    return h
```

```python
import functools
import math

import jax
import jax.numpy as jnp
from jax import lax
from jax.experimental import pallas as pl
from jax.experimental.pallas import tpu as pltpu

EPS = 1e-6
GLA_GATE_NORMALIZER = 16.0
LANES = 128
VMEM_LIMIT = 56 * 1024 * 1024
LOG2E = math.log2(math.e)

F32 = jnp.float32
BF16 = jnp.bfloat16


def _rms(x, w):
    return x * lax.rsqrt(jnp.mean(x * x, axis=-1, keepdims=True) + EPS) * w


def _sigmoid(x):
    return 1.0 / (1.0 + jnp.exp(-x))


def _dot(a, b):
    return jnp.dot(a, b, preferred_element_type=F32)


def _dot_nt(a, b):
    return lax.dot_general(a, b, (((1,), (1,)), ((), ())), preferred_element_type=F32)


def _dot_tn(a, b):
    return lax.dot_general(a, b, (((0,), (0,)), ((), ())), preferred_element_type=F32)


def _store_heads(ref, val, heads):
    d = val.shape[-1] // heads
    for h in range(heads):
        ref[h] = val[:, h * d:(h + 1) * d].astype(ref.dtype)


def _hg_proj_kernel(slot, heads, x_ref, nw_ref, lbl_ref, w_ref, q_ref, k_ref, v_ref, lf_ref, g_ref):
    y = _rms(x_ref[...], nw_ref[...]).astype(BF16)
    dim = g_ref.shape[-1]
    dk = dim // heads
    sect = lambda s: _dot(y, w_ref[:, s * dim:(s + 1) * dim])
    l = lbl_ref[...]
    e = jnp.exp(l - jnp.max(l, axis=0, keepdims=True))
    lb = jnp.sum(e[:slot + 1], axis=0, keepdims=True) / jnp.sum(e, axis=0, keepdims=True)
    q = sect(0)
    _store_heads(q_ref, q * _sigmoid(q) * (dk ** -0.5), heads)
    forget = lb + (1.0 - lb) * _sigmoid(sect(1))
    _store_heads(k_ref, 1.0 - forget, heads)
    _store_heads(lf_ref, jnp.log(forget) * LOG2E, heads)
    _store_heads(v_ref, sect(2), heads)
    g_ref[...] = sect(3)


def _gla_proj_kernel(heads, kd, vd, x_ref, nw_ref, w_ref, wlr_ref, wup_ref, bup_ref,
                     q_ref, k_ref, v_ref, lf_ref, g_ref):
    y = _rms(x_ref[...], nw_ref[...]).astype(BF16)
    dk = kd // heads
    _store_heads(q_ref, _dot(y, w_ref[:, 0:kd]) * (dk ** -0.5), heads)
    _store_heads(k_ref, _dot(y, w_ref[:, kd:2 * kd]), heads)
    _store_heads(v_ref, _dot(y, w_ref[:, 2 * kd:2 * kd + vd]), heads)
    g_ref[...] = _dot(y, w_ref[:, 2 * kd + vd:2 * kd + 2 * vd])
    lr = _dot(y, wlr_ref[...]).astype(BF16)
    gk = _dot(lr, wup_ref[...]) + bup_ref[...]
    log_sig = jnp.minimum(gk, 0.0) - jnp.log(1.0 + jnp.exp(-jnp.abs(gk)))
    _store_heads(lf_ref, log_sig * (LOG2E / GLA_GATE_NORMALIZER), heads)


SUBLANES = 8


def _row_blocks(x):
    return [x[SUBLANES * j:SUBLANES * (j + 1)] for j in range(x.shape[0] // SUBLANES)]


def _bcast_row(blk, r):
    return jnp.broadcast_to(blk[r:r + 1], blk.shape)


def _small_level(m, sub, cb, qb, kb):
    upper = (sub & m) != 0
    qt, kt, cn = [], [], []
    for c, q, k in zip(cb, qb, kb):
        if m == 1:
            r = jnp.where(upper, pltpu.roll(c, 1, axis=0), c)
        elif m == 2:
            r = jnp.where(sub < 4, _bcast_row(c, 1), _bcast_row(c, 5))
        else:
            r = _bcast_row(c, 3)
        f = jnp.exp2(jnp.where(upper, c, r - c))
        qt.append(q * f)
        kt.append(k * f)
        cn.append(jnp.where(upper, c + r, c))
    return qt, kt, cn


def _big_level(m, cb, qb, kb):
    nb = m // SUBLANES
    zero = jnp.zeros_like(cb[0])
    qt, kt, cn = [], [], []
    for i in range(len(cb) // (2 * nb)):
        lo = range(2 * nb * i, 2 * nb * i + nb)
        r = _bcast_row(cb[lo[-1]], SUBLANES - 1)
        for j in lo:
            qt.append(zero)
            kt.append(kb[j] * jnp.exp2(r - cb[j]))
            cn.append(cb[j])
        for j in range(lo[-1] + 1, lo[-1] + 1 + nb):
            qt.append(qb[j] * jnp.exp2(cb[j]))
            kt.append(zero)
            cn.append(cb[j] + r)
    return qt, kt, cn


def _attn_kernel(heads, group, q_ref, k_ref, v_ref, lf_ref, o_ref, st_ref):
    @pl.when(pl.program_id(1) == 0)
    def _():
        st_ref[...] = jnp.zeros_like(st_ref)

    C, dk = q_ref.shape[1], q_ref.shape[2]
    n_small = int(math.log2(SUBLANES))
    sub = lax.broadcasted_iota(jnp.int32, (SUBLANES, dk), 0)
    lane = lax.broadcasted_iota(jnp.int32, (SUBLANES, C), 1)
    ti = lax.broadcasted_iota(jnp.int32, (C, C), 0)
    si = lax.broadcasted_iota(jnp.int32, (C, C), 1)
    x = ti ^ si
    small_id = jnp.full((C, C), -1, jnp.int32)
    for l in reversed(range(n_small)):
        small_id = jnp.where(x < 2 ** (l + 1), l + 1, small_id)
    small_id = jnp.where(x == 0, 0, jnp.where((ti < si) | (x >= SUBLANES), -1, small_id))
    small_id_b = _row_blocks(small_id)

    def head_group(hg, carry):
        hs = [hg * group + i for i in range(group)]
        qs = [q_ref[h] for h in hs]
        ks = [k_ref[h] for h in hs]
        qb = [_row_blocks(q) for q in qs]
        kb = [_row_blocks(k) for k in ks]
        cb = [_row_blocks(lf_ref[h]) for h in hs]
        ab = [None] * group

        scores = lambda qt, kt: _dot_nt(jnp.concatenate(qt, axis=0).astype(BF16),
                                        jnp.concatenate(kt, axis=0).astype(BF16))
        ps = [[scores(qb[i], kb[i])] for i in range(group)]
        for l in range(n_small):
            for i in range(group):
                qt, kt, cb[i] = _small_level(2 ** l, sub, cb[i], qb[i], kb[i])
                ps[i].append(scores(qt, kt))
        for i in range(group):
            pb = [_row_blocks(p) for p in ps[i]]
            rows = []
            for j, sid in enumerate(small_id_b):
                a = jnp.zeros_like(pb[0][j])
                for l in reversed(range(n_small + 1)):
                    a = jnp.where(sid == l, pb[l][j], a)
                rows.append(a)
            ab[i] = rows
        m = SUBLANES
        while m < C:
            for i in range(group):
                qt, kt, cb[i] = _big_level(m, cb[i], qb[i], kb[i])
                pb = _row_blocks(scores(qt, kt))
                nb = m // SUBLANES
                for blk in range(C // (2 * m)):
                    keep = (lane >= 2 * m * blk) & (lane < 2 * m * blk + m)
                    for j in range(2 * nb * blk + nb, 2 * nb * (blk + 1)):
                        ab[i][j] = jnp.where(keep, pb[j], ab[i][j])
            m *= 2
        for i, h in enumerate(hs):
            c = jnp.concatenate(cb[i], axis=0)
            a = jnp.concatenate(ab[i], axis=0).astype(BF16)
            v = v_ref[h]
            st = st_ref[h]
            o_ref[h] = _dot(a, v) + _dot_nt((qs[i] * jnp.exp2(c)).astype(BF16), st.astype(BF16))
            c_last = c[C - 1:C, :]
            kd = (ks[i] * jnp.exp2(c_last - c)).astype(BF16)
            st_ref[h] = st * jnp.exp2(c_last) + _dot_tn(v, kd)
        return carry

    lax.fori_loop(0, heads // group, head_group, 0)


def _dense_kernel(heads, final, o_ref, g_ref, hnw_ref, wout_ref, h_ref, fnw_ref, wup_ref, cw_ref,
                  cb_ref, wdown_ref, finw_ref, out_ref, carry_ref):
    tm = h_ref.shape[0]
    dff = wdown_ref.shape[0]

    @pl.when(pl.program_id(1) == 0)
    def _():
        carry_ref[...] = jnp.zeros_like(carry_ref)

    hw = hnw_ref[...]
    parts = []
    for hh in range(heads):
        o = o_ref[hh]
        parts.append(o * lax.rsqrt(jnp.mean(o * o, axis=-1, keepdims=True) + EPS) * hw)
    g = g_ref[...]
    mix_in = (jnp.concatenate(parts, axis=-1) * (g * _sigmoid(g))).astype(BF16)
    h = h_ref[...] + _dot(mix_in, wout_ref[...])

    y = _rms(h, fnw_ref[...]).astype(BF16)
    slab = 2 * LANES
    n_slabs = dff // slab
    up = lambda n: [_dot(y, wup_ref[:, base:base + slab]) for base in (n * slab, dff + n * slab)]

    sub = lax.broadcasted_iota(jnp.int32, (SUBLANES, slab), 0)

    def delayed(blocks):
        rot = [pltpu.roll(b, 1, axis=0) for b in blocks]
        return [rot[0]] + [jnp.where(sub < 1, rot[i - 1], rot[i]) for i in range(1, len(rot))]

    def conv(u, base):
        blocks = [carry_ref[:, base:base + slab]] + _row_blocks(u)
        carry_ref[:, base:base + slab] = blocks[-1]
        cw = cw_ref[:, base:base + slab]
        w0, w1, w2, b = cw[0:1], cw[1:2], cw[2:3], cb_ref[:, base:base + slab]
        inner = [w1 * x + d for x, d in zip(blocks, delayed([w0 * x for x in blocks]))]
        return jnp.concatenate([b + w2 * x + d for x, d in zip(blocks[1:], delayed(inner)[1:])], axis=0)

    acc = h
    u_next = up(0)
    for n in range(n_slabs):
        u_a, u_gate = u_next
        if n + 1 < n_slabs:
            u_next = up(n + 1)
        a, gate = conv(u_a, n * slab), conv(u_gate, dff + n * slab)
        s = (gate * _sigmoid(gate) * a).astype(BF16)
        acc = acc + _dot(s, wdown_ref[n * slab:(n + 1) * slab, :])
    if final:
        acc = _rms(acc, finw_ref[...])
    out_ref[...] = acc


def _const_spec(shape):
    nd = len(shape)
    return pl.BlockSpec(shape, lambda *_: (0,) * nd, pipeline_mode=pl.Buffered(1))


def _params(sem):
    return pltpu.CompilerParams(dimension_semantics=sem, vmem_limit_bytes=VMEM_LIMIT)


def _head_out(B, T, heads, d, tm, dtype=F32):
    return (jax.ShapeDtypeStruct((B, heads, T, d), dtype),
            pl.BlockSpec((None, heads, tm, d), lambda b, i: (b, 0, i, 0)))


def _hg_proj(h, nw, lb_logits, w_in, slot, heads, tm):
    B, T, D = h.shape
    dim = w_in.shape[1] // 4
    dk = dim // heads
    hs, hspec = _head_out(B, T, heads, dk, tm)
    vs, _ = _head_out(B, T, heads, dk, tm, BF16)
    tok_spec = pl.BlockSpec((None, tm, dim), lambda b, i: (b, i, 0))
    return pl.pallas_call(
        functools.partial(_hg_proj_kernel, slot, heads),
        grid=(B, T // tm),
        in_specs=[pl.BlockSpec((None, tm, D), lambda b, i: (b, i, 0)), _const_spec((1, D)),
                  _const_spec(lb_logits.shape), _const_spec(w_in.shape)],
        out_specs=[hspec, hspec, hspec, hspec, tok_spec],
        out_shape=[hs, hs, vs, hs, jax.ShapeDtypeStruct((B, T, dim), F32)],
        compiler_params=_params(("parallel", "parallel")),
        name="hg_proj",
    )(h, nw.reshape(1, D), lb_logits, w_in)


def _gla_proj(h, nw, w_main, w_lr, w_up, b_up, heads, kd, vd, tm):
    B, T, D = h.shape
    ks, kspec = _head_out(B, T, heads, kd // heads, tm)
    vs, vspec = _head_out(B, T, heads, vd // heads, tm, BF16)
    tok_spec = pl.BlockSpec((None, tm, vd), lambda b, i: (b, i, 0))
    return pl.pallas_call(
        functools.partial(_gla_proj_kernel, heads, kd, vd),
        grid=(B, T // tm),
        in_specs=[pl.BlockSpec((None, tm, D), lambda b, i: (b, i, 0)), _const_spec((1, D)),
                  _const_spec(w_main.shape), _const_spec(w_lr.shape), _const_spec(w_up.shape),
                  _const_spec((1, kd))],
        out_specs=[kspec, kspec, vspec, kspec, tok_spec],
        out_shape=[ks, ks, vs, ks, jax.ShapeDtypeStruct((B, T, vd), F32)],
        compiler_params=_params(("parallel", "parallel")),
        name="gla_proj",
    )(h, nw.reshape(1, D), w_main, w_lr, w_up, b_up.reshape(1, kd))


def _attention(q, k, v, lf, chunk, group):
    B, H, T, dk = q.shape
    dv = v.shape[-1]
    kspec = pl.BlockSpec((None, H, chunk, dk), lambda b, c: (b, 0, c, 0))
    vspec = pl.BlockSpec((None, H, chunk, dv), lambda b, c: (b, 0, c, 0))
    return pl.pallas_call(
        functools.partial(_attn_kernel, H, group),
        grid=(B, T // chunk),
        in_specs=[kspec, kspec, vspec, kspec],
        out_specs=vspec,
        out_shape=jax.ShapeDtypeStruct((B, H, T, dv), F32),
        scratch_shapes=[pltpu.VMEM((H, dv, dk), F32)],
        compiler_params=_params(("parallel", "arbitrary")),
        name="chunk_attn",
    )(q, k, v, lf)


def _dense(o, g, hnw, w_out, h, fnw, w_up, conv_w, conv_b, w_down, finw, final, tm):
    B, H, T, dv = o.shape
    D = h.shape[-1]
    dff = w_down.shape[0]
    tok = lambda d: pl.BlockSpec((None, tm, d), lambda b, i: (b, i, 0))
    return pl.pallas_call(
        functools.partial(_dense_kernel, H, final),
        grid=(B, T // tm),
        in_specs=[pl.BlockSpec((None, H, tm, dv), lambda b, i: (b, 0, i, 0)), tok(H * dv),
                  _const_spec((1, dv)), _const_spec(w_out.shape), tok(D), _const_spec((1, D)),
                  _const_spec(w_up.shape), _const_spec(conv_w.shape), _const_spec((1, 2 * dff)),
                  _const_spec(w_down.shape), _const_spec((1, D))],
        out_specs=tok(D),
        out_shape=jax.ShapeDtypeStruct((B, T, D), F32),
        scratch_shapes=[pltpu.VMEM((8, 2 * dff), F32)],
        compiler_params=_params(("parallel", "arbitrary")),
        name="dense_tail",
    )(o, g, hnw.reshape(1, dv), w_out, h, fnw.reshape(1, D), w_up, conv_w, conv_b.reshape(1, 2 * dff),
      w_down, finw.reshape(1, D))


def kernel(x, lb_logits, hg_w_in, hg_norm_w, hg_w_out, gla_w_in, gla_w_gk_up, gla_b_gk_up, gla_norm_w,
           gla_w_out, norm_mixer_w, norm_ffn_w, ffn_w_up, ffn_conv_w, ffn_conv_b, ffn_w_down, norm_final_w):
    depth = norm_mixer_w.shape[0]
    hg_heads = hg_w_out.shape[1] // hg_norm_w.shape[-1]
    gla_vd = gla_w_out.shape[1]
    gla_heads = gla_vd // gla_norm_w.shape[-1]
    gla_kd = gla_w_gk_up.shape[-1]
    rank = gla_w_gk_up.shape[1]
    chunk = 128
    tm_proj = 512
    tm_dense = 256
    bf = lambda w: w.astype(BF16)

    h = x
    for i in range(depth):
        j = i // 2
        if i % 2 == 0:
            q, k, v, lf, g = _hg_proj(h, norm_mixer_w[i], lb_logits, bf(hg_w_in[j]), i, hg_heads, tm_proj)
            hnw, w_out = hg_norm_w[j], hg_w_out[j]
        else:
            n_main = 2 * gla_kd + 2 * gla_vd
            w_in = gla_w_in[j]
            w_lr = jnp.pad(w_in[:, n_main:], ((0, 0), (0, LANES - rank)))
            w_up = jnp.pad(gla_w_gk_up[j], ((0, LANES - rank), (0, 0)))
            q, k, v, lf, g = _gla_proj(h, norm_mixer_w[i], bf(w_in[:, :n_main]), bf(w_lr), bf(w_up),
                                       gla_b_gk_up[j], gla_heads, gla_kd, gla_vd, tm_proj)
            hnw, w_out = gla_norm_w[j], gla_w_out[j]
        o = _attention(q, k, v, lf, chunk, 4)
        h = _dense(o, g, hnw, bf(w_out), h, norm_ffn_w[i], bf(ffn_w_up[i]), ffn_conv_w[i], ffn_conv_b[i],
                   bf(ffn_w_down[i]), norm_final_w, i == depth - 1, tm_dense)
    return h
```

```python
import functools
import math

import jax
import jax.numpy as jnp
from jax import lax
from jax.experimental import pallas as pl
from jax.experimental.pallas import tpu as pltpu

EPS = 1e-6
GLA_GATE_NORMALIZER = 16.0
LANES = 128
SUBLANES = 8
VMEM_LIMIT = 56 * 1024 * 1024
LOG2E = math.log2(math.e)
CHUNK = 128
HEAD_GROUP = 4
TM_DENSE = 256

F32 = jnp.float32
BF16 = jnp.bfloat16


def _rms(x, w):
    return x * lax.rsqrt(jnp.mean(x * x, axis=-1, keepdims=True) + EPS) * w


def _sigmoid(x):
    return 1.0 / (1.0 + jnp.exp(-x))


def _dot(a, b):
    return jnp.dot(a, b, preferred_element_type=F32)


def _dot_nt(a, b):
    return lax.dot_general(a, b, (((1,), (1,)), ((), ())), preferred_element_type=F32)


def _dot_tn(a, b):
    return lax.dot_general(a, b, (((0,), (0,)), ((), ())), preferred_element_type=F32)


def _dot_cols(y, w_ref, start, stop):
    slab = 2 * LANES
    return jnp.concatenate([_dot(y, w_ref[:, c:min(c + slab, stop)]) for c in range(start, stop, slab)], axis=-1)


def _store_heads(ref, slot, val):
    heads = ref.shape[1]
    d = val.shape[-1] // heads
    for h in range(heads):
        ref[slot, h] = val[:, h * d:(h + 1) * d].astype(ref.dtype)


def _row_blocks(x):
    return [x[SUBLANES * j:SUBLANES * (j + 1)] for j in range(x.shape[0] // SUBLANES)]


def _bcast_row(blk, r):
    return jnp.broadcast_to(blk[r:r + 1], blk.shape)


def _hg_proj_sections(slot_id, y, w_ref, lbl_ref, q_s, k_s, v_s, lf_s, g_ref, slot):
    dim = g_ref.shape[-1]
    dk = q_s.shape[-1]
    sect = lambda s: _dot_cols(y, w_ref, s * dim, (s + 1) * dim)

    def sec_q():
        q = sect(0)
        _store_heads(q_s, slot, q * _sigmoid(q) * (dk ** -0.5))

    def sec_f():
        l = lbl_ref[...]
        e = jnp.exp(l - jnp.max(l, axis=0, keepdims=True))
        lb = jnp.sum(e[:slot_id + 1], axis=0, keepdims=True) / jnp.sum(e, axis=0, keepdims=True)
        forget = lb + (1.0 - lb) * _sigmoid(sect(1))
        _store_heads(k_s, slot, 1.0 - forget)
        _store_heads(lf_s, slot, jnp.log(forget) * LOG2E)

    def sec_v():
        _store_heads(v_s, slot, sect(2))

    def sec_g():
        g_ref[...] = sect(3)

    return [sec_q, sec_f, sec_v, sec_g]


def _gla_proj_sections(y, w_ref, wlr_ref, wup_ref, bup_ref, q_s, k_s, v_s, lf_s, g_ref, slot):
    kd = q_s.shape[1] * q_s.shape[3]
    vd = g_ref.shape[-1]
    dk = q_s.shape[-1]

    def sec_qk():
        qk = _dot_cols(y, w_ref, 0, 2 * kd)
        _store_heads(q_s, slot, qk[:, :kd] * (dk ** -0.5))
        _store_heads(k_s, slot, qk[:, kd:])

    def sec_v():
        _store_heads(v_s, slot, _dot_cols(y, w_ref, 2 * kd, 2 * kd + vd))

    def sec_g():
        g_ref[...] = _dot_cols(y, w_ref, 2 * kd + vd, 2 * kd + 2 * vd)

    def sec_lf():
        lr = _dot(y, wlr_ref[...]).astype(BF16)
        gk = _dot(lr, wup_ref[...]) + bup_ref[...]
        log_sig = jnp.minimum(gk, 0.0) - jnp.log(1.0 + jnp.exp(-jnp.abs(gk)))
        _store_heads(lf_s, slot, log_sig * (LOG2E / GLA_GATE_NORMALIZER))

    return [sec_qk, sec_lf, sec_v, sec_g]


def _small_level(m, sub, cb, qb, kb):
    upper = (sub & m) != 0
    qt, kt, cn = [], [], []
    for c, q, k in zip(cb, qb, kb):
        if m == 1:
            r = jnp.where(upper, pltpu.roll(c, 1, axis=0), c)
        elif m == 2:
            r = jnp.where(sub < 4, _bcast_row(c, 1), _bcast_row(c, 5))
        else:
            r = _bcast_row(c, 3)
        f = jnp.exp2(jnp.where(upper, c, r - c))
        qt.append(q * f)
        kt.append(k * f)
        cn.append(jnp.where(upper, c + r, c))
    return qt, kt, cn


def _big_level(m, cb, qb, kb):
    nb = m // SUBLANES
    zero = jnp.zeros_like(cb[0])
    qt, kt, cn = [], [], []
    for i in range(len(cb) // (2 * nb)):
        lo = range(2 * nb * i, 2 * nb * i + nb)
        r = _bcast_row(cb[lo[-1]], SUBLANES - 1)
        for j in lo:
            qt.append(zero)
            kt.append(kb[j] * jnp.exp2(r - cb[j]))
            cn.append(cb[j])
        for j in range(lo[-1] + 1, lo[-1] + 1 + nb):
            qt.append(qb[j] * jnp.exp2(cb[j]))
            kt.append(zero)
            cn.append(cb[j] + r)
    return qt, kt, cn


def _attn_rounds(hs, slot, fresh, q_s, k_s, v_s, lf_s, o_ref, st_ref):
    group = len(hs)
    C, dk = q_s.shape[2], q_s.shape[3]
    n_small = int(math.log2(SUBLANES))
    sub = lax.broadcasted_iota(jnp.int32, (SUBLANES, dk), 0)
    lane = lax.broadcasted_iota(jnp.int32, (SUBLANES, C), 1)
    ti = lax.broadcasted_iota(jnp.int32, (C, C), 0)
    si = lax.broadcasted_iota(jnp.int32, (C, C), 1)
    x = ti ^ si
    small_id = jnp.full((C, C), -1, jnp.int32)
    for l in reversed(range(n_small)):
        small_id = jnp.where(x < 2 ** (l + 1), l + 1, small_id)
    small_id = jnp.where(x == 0, 0, jnp.where((ti < si) | (x >= SUBLANES), -1, small_id))
    small_id_b = _row_blocks(small_id)

    qs = [q_s[slot, h] for h in hs]
    ks = [k_s[slot, h] for h in hs]
    qb = [_row_blocks(q) for q in qs]
    kb = [_row_blocks(k) for k in ks]
    cb = [_row_blocks(lf_s[slot, h]) for h in hs]
    ab = [None] * group

    scores = lambda qt, kt: _dot_nt(jnp.concatenate(qt, axis=0).astype(BF16),
                                    jnp.concatenate(kt, axis=0).astype(BF16))
    ps = [[scores(qb[i], kb[i])] for i in range(group)]
    yield
    for l in range(n_small):
        for i in range(group):
            qt, kt, cb[i] = _small_level(2 ** l, sub, cb[i], qb[i], kb[i])
            ps[i].append(scores(qt, kt))
        yield
    for i in range(group):
        pb = [_row_blocks(p) for p in ps[i]]
        rows = []
        for j, sid in enumerate(small_id_b):
            a = jnp.zeros_like(pb[0][j])
            for l in reversed(range(n_small + 1)):
                a = jnp.where(sid == l, pb[l][j], a)
            rows.append(a)
        ab[i] = rows
    m = SUBLANES
    while m < C:
        for i in range(group):
            qt, kt, cb[i] = _big_level(m, cb[i], qb[i], kb[i])
            pb = _row_blocks(scores(qt, kt))
            nb = m // SUBLANES
            for blk in range(C // (2 * m)):
                keep = (lane >= 2 * m * blk) & (lane < 2 * m * blk + m)
                for j in range(2 * nb * blk + nb, 2 * nb * (blk + 1)):
                    ab[i][j] = jnp.where(keep, pb[j], ab[i][j])
        m *= 2
        yield
    for i, h in enumerate(hs):
        c = jnp.concatenate(cb[i], axis=0)
        a = jnp.concatenate(ab[i], axis=0).astype(BF16)
        v = v_s[slot, h]
        st = jnp.where(fresh, 0.0, st_ref[h])
        o_ref[h] = _dot(a, v) + _dot_nt((qs[i] * jnp.exp2(c)).astype(BF16), st.astype(BF16))
        c_last = c[C - 1:C, :]
        kd = (ks[i] * jnp.exp2(c_last - c)).astype(BF16)
        st_ref[h] = st * jnp.exp2(c_last) + _dot_tn(v, kd)
    yield


def _mixer_kernel(kind, slot_id, n_chunks, x_ref, nw_ref, *refs):
    n_w = 2 if kind == "hg" else 4
    w_refs, (o_ref, g_ref, q_s, k_s, v_s, lf_s, st_ref) = refs[:n_w], refs[n_w:]
    i = pl.program_id(0)

    @pl.when(i == 0)
    def _():
        for r in (q_s, k_s, v_s, lf_s, st_ref):
            r[...] = jnp.zeros_like(r)

    slot = lax.rem(i, 2)
    y = _rms(x_ref[...], nw_ref[...]).astype(BF16)
    if kind == "hg":
        sections = _hg_proj_sections(slot_id, y, w_refs[1], w_refs[0], q_s, k_s, v_s, lf_s, g_ref, slot)
    else:
        sections = _gla_proj_sections(y, *w_refs, q_s, k_s, v_s, lf_s, g_ref, slot)

    prev = jnp.maximum(i - 1, 0)
    fresh = lax.rem(prev, n_chunks) == 0
    heads = q_s.shape[1]
    groups = [list(range(g0, min(g0 + HEAD_GROUP, heads))) for g0 in range(0, heads, HEAD_GROUP)]
    rounds_per_group = 2 + int(math.log2(CHUNK))
    stride = 1
    while sections:
        sections.pop(0)()
    done = 0
    for hs in groups:
        for _ in _attn_rounds(hs, 1 - slot, fresh, q_s, k_s, v_s, lf_s, o_ref, st_ref):
            if sections and done % stride == 0:
                sections.pop(0)()
            done += 1
    for sec in sections:
        sec()


def _dense_kernel(heads, final, o_ref, g_ref, hnw_ref, wout_ref, h_ref, fnw_ref, wup_ref, cw_ref,
                  cb_ref, wdown_ref, finw_ref, out_ref, carry_ref):
    dff = wdown_ref.shape[0]

    @pl.when(pl.program_id(1) == 0)
    def _():
        carry_ref[...] = jnp.zeros_like(carry_ref)

    hw = hnw_ref[...]
    parts = []
    for hh in range(heads):
        o = o_ref[hh]
        parts.append(o * lax.rsqrt(jnp.mean(o * o, axis=-1, keepdims=True) + EPS) * hw)
    g = g_ref[...]
    mix_in = (jnp.concatenate(parts, axis=-1) * (g * _sigmoid(g))).astype(BF16)
    h = h_ref[...] + _dot(mix_in, wout_ref[...])

    y = _rms(h, fnw_ref[...]).astype(BF16)
    slab = 2 * LANES
    n_slabs = dff // slab
    up = lambda n: [_dot(y, wup_ref[:, base:base + slab]) for base in (n * slab, dff + n * slab)]

    sub = lax.broadcasted_iota(jnp.int32, (SUBLANES, slab), 0)

    def delayed(blocks):
        rot = [pltpu.roll(b, 1, axis=0) for b in blocks]
        return [rot[0]] + [jnp.where(sub < 1, rot[i - 1], rot[i]) for i in range(1, len(rot))]

    def conv(u, base):
        blocks = [carry_ref[:, base:base + slab]] + _row_blocks(u)
        carry_ref[:, base:base + slab] = blocks[-1]
        cw = cw_ref[:, base:base + slab]
        w0, w1, w2, b = cw[0:1], cw[1:2], cw[2:3], cb_ref[:, base:base + slab]
        inner = [w1 * x + d for x, d in zip(blocks, delayed([w0 * x for x in blocks]))]
        return jnp.concatenate([b + w2 * x + d for x, d in zip(blocks[1:], delayed(inner)[1:])], axis=0)

    acc = h
    u_next = up(0)
    for n in range(n_slabs):
        u_a, u_gate = u_next
        if n + 1 < n_slabs:
            u_next = up(n + 1)
        a, gate = conv(u_a, n * slab), conv(u_gate, dff + n * slab)
        s = (gate * _sigmoid(gate) * a).astype(BF16)
        acc = acc + _dot(s, wdown_ref[n * slab:(n + 1) * slab, :])
    if final:
        acc = _rms(acc, finw_ref[...])
    out_ref[...] = acc


def _const_spec(shape):
    nd = len(shape)
    return pl.BlockSpec(shape, lambda *_: (0,) * nd, pipeline_mode=pl.Buffered(1))


def _params(sem):
    return pltpu.CompilerParams(dimension_semantics=sem, vmem_limit_bytes=VMEM_LIMIT)


def _mixer(kind, slot_id, h, nw, weights, heads, dk, dv):
    B, T, D = h.shape
    n_chunks = T // CHUNK
    n = B * n_chunks
    cur = lambda i: jnp.minimum(i, n - 1)
    prv = lambda i: jnp.maximum(i - 1, 0)
    tok_map = lambda i: (cur(i) // n_chunks, cur(i) % n_chunks, 0)
    return pl.pallas_call(
        functools.partial(_mixer_kernel, kind, slot_id, n_chunks),
        grid=(n + 1,),
        in_specs=[pl.BlockSpec((None, CHUNK, D), tok_map), _const_spec((1, D))]
        + [_const_spec(w.shape) for w in weights],
        out_specs=[pl.BlockSpec((None, heads, CHUNK, dv), lambda i: (prv(i) // n_chunks, 0, prv(i) % n_chunks, 0)),
                   pl.BlockSpec((None, CHUNK, heads * dv), tok_map)],
        out_shape=[jax.ShapeDtypeStruct((B, heads, T, dv), F32),
                   jax.ShapeDtypeStruct((B, T, heads * dv), F32)],
        scratch_shapes=[pltpu.VMEM((2, heads, CHUNK, dk), F32), pltpu.VMEM((2, heads, CHUNK, dk), F32),
                        pltpu.VMEM((2, heads, CHUNK, dv), BF16), pltpu.VMEM((2, heads, CHUNK, dk), F32),
                        pltpu.VMEM((heads, dv, dk), F32)],
        compiler_params=_params(("arbitrary",)),
        name=kind + "_mixer",
    )(h, nw.reshape(1, D), *weights)


def _dense(o, g, hnw, w_out, h, fnw, w_up, conv_w, conv_b, w_down, finw, final):
    B, H, T, dv = o.shape
    D = h.shape[-1]
    dff = w_down.shape[0]
    tm = TM_DENSE
    tok = lambda d: pl.BlockSpec((None, tm, d), lambda b, i: (b, i, 0))
    return pl.pallas_call(
        functools.partial(_dense_kernel, H, final),
        grid=(B, T // tm),
        in_specs=[pl.BlockSpec((None, H, tm, dv), lambda b, i: (b, 0, i, 0)), tok(H * dv),
                  _const_spec((1, dv)), _const_spec(w_out.shape), tok(D), _const_spec((1, D)),
                  _const_spec(w_up.shape), _const_spec(conv_w.shape), _const_spec((1, 2 * dff)),
                  _const_spec(w_down.shape), _const_spec((1, D))],
        out_specs=tok(D),
        out_shape=jax.ShapeDtypeStruct((B, T, D), F32),
        scratch_shapes=[pltpu.VMEM((SUBLANES, 2 * dff), F32)],
        compiler_params=_params(("parallel", "arbitrary")),
        name="dense_tail",
    )(o, g, hnw.reshape(1, dv), w_out, h, fnw.reshape(1, D), w_up, conv_w, conv_b.reshape(1, 2 * dff),
      w_down, finw.reshape(1, D))


def kernel(x, lb_logits, hg_w_in, hg_norm_w, hg_w_out, gla_w_in, gla_w_gk_up, gla_b_gk_up, gla_norm_w,
           gla_w_out, norm_mixer_w, norm_ffn_w, ffn_w_up, ffn_conv_w, ffn_conv_b, ffn_w_down, norm_final_w):
    depth = norm_mixer_w.shape[0]
    hg_dk = hg_norm_w.shape[-1]
    hg_heads = hg_w_out.shape[1] // hg_dk
    gla_vd = gla_w_out.shape[1]
    gla_dv = gla_norm_w.shape[-1]
    gla_heads = gla_vd // gla_dv
    gla_kd = gla_w_gk_up.shape[-1]
    rank = gla_w_gk_up.shape[1]
    bf = lambda w: w.astype(BF16)

    h = x
    for i in range(depth):
        j = i // 2
        if i % 2 == 0:
            o, g = _mixer("hg", i, h, norm_mixer_w[i], [lb_logits, bf(hg_w_in[j])], hg_heads, hg_dk, hg_dk)
            hnw, w_out = hg_norm_w[j], hg_w_out[j]
        else:
            n_main = 2 * gla_kd + 2 * gla_vd
            w_in = gla_w_in[j]
            w_lr = jnp.pad(w_in[:, n_main:], ((0, 0), (0, LANES - rank)))
            w_up = jnp.pad(gla_w_gk_up[j], ((0, LANES - rank), (0, 0)))
            weights = [bf(w_in[:, :n_main]), bf(w_lr), bf(w_up), gla_b_gk_up[j].reshape(1, gla_kd)]
            o, g = _mixer("gla", i, h, norm_mixer_w[i], weights, gla_heads, gla_kd // gla_heads, gla_dv)
            hnw, w_out = gla_norm_w[j], gla_w_out[j]
        h = _dense(o, g, hnw, bf(w_out), h, norm_ffn_w[i], bf(ffn_w_up[i]), ffn_conv_w[i], ffn_conv_b[i],
                   bf(ffn_w_down[i]), norm_final_w, i == depth - 1)
    return h
```

```python
import functools
import math

import jax
import jax.numpy as jnp
from jax import lax
from jax.experimental import pallas as pl
from jax.experimental.pallas import tpu as pltpu

EPS = 1e-6
GLA_GATE_NORMALIZER = 16.0
LANES = 128
SUBLANES = 8
VMEM_LIMIT = 56 * 1024 * 1024
LOG2E = math.log2(math.e)
CHUNK = 128
CHUNKS_PER_STEP = 4
LOCKSTEP = 8
TM_PROJ = 512
TM_DENSE = 256

F32 = jnp.float32
BF16 = jnp.bfloat16


def _rms(x, w):
    return x * lax.rsqrt(jnp.mean(x * x, axis=-1, keepdims=True) + EPS) * w


def _sigmoid(x):
    return 1.0 / (1.0 + jnp.exp(-x))


def _dot(a, b):
    return jnp.dot(a, b, preferred_element_type=F32)


def _dot_nt(a, b):
    return lax.dot_general(a, b, (((1,), (1,)), ((), ())), preferred_element_type=F32)


def _dot_tn(a, b):
    return lax.dot_general(a, b, (((0,), (0,)), ((), ())), preferred_element_type=F32)


def _store_heads(ref, val, heads):
    d = val.shape[-1] // heads
    for h in range(heads):
        ref[h] = val[:, h * d:(h + 1) * d].astype(ref.dtype)


def _row_blocks(x):
    return [x[SUBLANES * j:SUBLANES * (j + 1)] for j in range(x.shape[0] // SUBLANES)]


def _bcast_row(blk, r):
    return jnp.broadcast_to(blk[r:r + 1], blk.shape)


def _hg_proj_kernel(slot, heads, x_ref, nw_ref, lbl_ref, w_ref, q_ref, k_ref, v_ref, lf_ref, g_ref):
    y = _rms(x_ref[...], nw_ref[...]).astype(BF16)
    dim = g_ref.shape[-1]
    dk = dim // heads
    sect = lambda s: _dot(y, w_ref[:, s * dim:(s + 1) * dim])
    l = lbl_ref[...]
    e = jnp.exp(l - jnp.max(l, axis=0, keepdims=True))
    lb = jnp.sum(e[:slot + 1], axis=0, keepdims=True) / jnp.sum(e, axis=0, keepdims=True)
    q = sect(0)
    _store_heads(q_ref, q * _sigmoid(q) * (dk ** -0.5), heads)
    forget = lb + (1.0 - lb) * _sigmoid(sect(1))
    _store_heads(k_ref, 1.0 - forget, heads)
    _store_heads(lf_ref, jnp.log(forget) * LOG2E, heads)
    _store_heads(v_ref, sect(2), heads)
    g_ref[...] = sect(3)


def _gla_proj_kernel(heads, kd, vd, x_ref, nw_ref, w_ref, wlr_ref, wup_ref, bup_ref,
                     q_ref, k_ref, v_ref, lf_ref, g_ref):
    y = _rms(x_ref[...], nw_ref[...]).astype(BF16)
    dk = kd // heads
    _store_heads(q_ref, _dot(y, w_ref[:, 0:kd]) * (dk ** -0.5), heads)
    _store_heads(k_ref, _dot(y, w_ref[:, kd:2 * kd]), heads)
    _store_heads(v_ref, _dot(y, w_ref[:, 2 * kd:2 * kd + vd]), heads)
    g_ref[...] = _dot(y, w_ref[:, 2 * kd + vd:2 * kd + 2 * vd])
    lr = _dot(y, wlr_ref[...]).astype(BF16)
    gk = _dot(lr, wup_ref[...]) + bup_ref[...]
    log_sig = jnp.minimum(gk, 0.0) - jnp.log(1.0 + jnp.exp(-jnp.abs(gk)))
    _store_heads(lf_ref, log_sig * (LOG2E / GLA_GATE_NORMALIZER), heads)


def _small_level(m, sub, cb, qb, kb):
    upper = (sub & m) != 0
    zs, cn = [], []
    for c, q, k in zip(cb, qb, kb):
        if m == 1:
            r = jnp.where(upper, pltpu.roll(c, 1, axis=0), c)
        elif m == 2:
            r = jnp.where(sub < 4, _bcast_row(c, 1), _bcast_row(c, 5))
        else:
            r = _bcast_row(c, 3)
        zs.append(jnp.where(upper, q, k) * jnp.exp2(jnp.where(upper, c, r - c)))
        cn.append(jnp.where(upper, c + r, c))
    return zs, cn


def _big_level(m, cb, qb, kb):
    nb = m // SUBLANES
    zs, cn = [], []
    for i in range(len(cb) // (2 * nb)):
        lo = range(2 * nb * i, 2 * nb * i + nb)
        r = _bcast_row(cb[lo[-1]], SUBLANES - 1)
        for j in lo:
            zs.append(kb[j] * jnp.exp2(r - cb[j]))
            cn.append(cb[j])
        for j in range(lo[-1] + 1, lo[-1] + 1 + nb):
            zs.append(qb[j] * jnp.exp2(cb[j]))
            cn.append(cb[j] + r)
    return zs, cn


def _attn_kernel(heads, q_ref, k_ref, v_ref, lf_ref, o_ref, st_ref):
    @pl.when(pl.program_id(1) == 0)
    def _():
        st_ref[...] = jnp.zeros_like(st_ref)

    C, dk = CHUNK, q_ref.shape[2]
    group = min(LOCKSTEP, heads)
    n_groups = heads // group
    span = max(1, LOCKSTEP // group)
    n_small = int(math.log2(SUBLANES))
    sub = lax.broadcasted_iota(jnp.int32, (SUBLANES, dk), 0)
    lane = lax.broadcasted_iota(jnp.int32, (SUBLANES, C), 1)
    ti = lax.broadcasted_iota(jnp.int32, (C, C), 0)
    si = lax.broadcasted_iota(jnp.int32, (C, C), 1)
    x = ti ^ si
    small_id = jnp.full((C, C), -1, jnp.int32)
    for l in reversed(range(n_small)):
        small_id = jnp.where(x < 2 ** (l + 1), l + 1, small_id)
    small_id = jnp.where(x == 0, 0, jnp.where((ti < si) | (x >= SUBLANES), -1, small_id))
    small_id_b = _row_blocks(small_id)

    def chunk_head_group(it, carry):
        first = (it // n_groups) * span
        members = [(pl.ds(pl.multiple_of((first + cc) * C, C), C), (it % n_groups) * group + i)
                   for cc in range(span) for i in range(group)]
        n = len(members)
        qs = [q_ref[h, rows, :] for rows, h in members]
        ks = [k_ref[h, rows, :] for rows, h in members]
        qb = [_row_blocks(q) for q in qs]
        kb = [_row_blocks(k) for k in ks]
        cb = [_row_blocks(lf_ref[h, rows, :]) for rows, h in members]
        ab = [None] * n

        def scores(zs):
            z = jnp.concatenate(zs, axis=0).astype(BF16)
            return _dot_nt(z, z)

        ps = [[_dot_nt(qs[i].astype(BF16), ks[i].astype(BF16))] for i in range(n)]
        for l in range(n_small):
            for i in range(n):
                zs, cb[i] = _small_level(2 ** l, sub, cb[i], qb[i], kb[i])
                ps[i].append(scores(zs))
        for i in range(n):
            pb = [_row_blocks(p) for p in ps[i]]
            blocks = []
            for j, sid in enumerate(small_id_b):
                a = jnp.zeros_like(pb[0][j])
                for l in reversed(range(n_small + 1)):
                    a = jnp.where(sid == l, pb[l][j], a)
                blocks.append(a)
            ab[i] = blocks
        m = SUBLANES
        while m < C:
            for i in range(n):
                zs, cb[i] = _big_level(m, cb[i], qb[i], kb[i])
                pb = _row_blocks(scores(zs))
                nb = m // SUBLANES
                for blk in range(C // (2 * m)):
                    keep = (lane >= 2 * m * blk) & (lane < 2 * m * blk + m)
                    for j in range(2 * nb * blk + nb, 2 * nb * (blk + 1)):
                        ab[i][j] = jnp.where(keep, pb[j], ab[i][j])
            m *= 2
        for i, (rows, h) in enumerate(members):
            c = jnp.concatenate(cb[i], axis=0)
            a = jnp.concatenate(ab[i], axis=0).astype(BF16)
            v = v_ref[h, rows, :]
            st = st_ref[h]
            o_ref[h, rows, :] = _dot(a, v) + _dot_nt((qs[i] * jnp.exp2(c)).astype(BF16), st.astype(BF16))
            c_last = c[C - 1:C, :]
            kd = (ks[i] * jnp.exp2(c_last - c)).astype(BF16)
            st_ref[h] = st * jnp.exp2(c_last) + _dot_tn(v, kd)
        return carry

    lax.fori_loop(0, (q_ref.shape[1] // (C * span)) * n_groups, chunk_head_group, 0)


def _dense_kernel(heads, final, o_ref, g_ref, hnw_ref, wout_ref, h_ref, fnw_ref, wup_ref, cw_ref,
                  cb_ref, wdown_ref, finw_ref, out_ref, carry_ref):
    dff = wdown_ref.shape[0]

    @pl.when(pl.program_id(1) == 0)
    def _():
        carry_ref[...] = jnp.zeros_like(carry_ref)

    hw = hnw_ref[...]
    parts = []
    for hh in range(heads):
        o = o_ref[hh]
        parts.append(o * lax.rsqrt(jnp.mean(o * o, axis=-1, keepdims=True) + EPS) * hw)
    g = g_ref[...]
    mix_in = (jnp.concatenate(parts, axis=-1) * (g * _sigmoid(g))).astype(BF16)
    h = h_ref[...] + _dot(mix_in, wout_ref[...])

    y = _rms(h, fnw_ref[...]).astype(BF16)
    slab = 2 * LANES
    n_slabs = dff // slab
    up = lambda n: [_dot(y, wup_ref[:, base:base + slab]) for base in (n * slab, dff + n * slab)]

    sub = lax.broadcasted_iota(jnp.int32, (SUBLANES, slab), 0)

    def delayed(blocks):
        rot = [pltpu.roll(b, 1, axis=0) for b in blocks]
        return [rot[0]] + [jnp.where(sub < 1, rot[i - 1], rot[i]) for i in range(1, len(rot))]

    def conv(u, base):
        blocks = [carry_ref[:, base:base + slab]] + _row_blocks(u)
        carry_ref[:, base:base + slab] = blocks[-1]
        cw = cw_ref[:, base:base + slab]
        w0, w1, w2, b = cw[0:1], cw[1:2], cw[2:3], cb_ref[:, base:base + slab]
        inner = [w1 * x + d for x, d in zip(blocks, delayed([w0 * x for x in blocks]))]
        return jnp.concatenate([b + w2 * x + d for x, d in zip(blocks[1:], delayed(inner)[1:])], axis=0)

    acc = h
    u_next = up(0)
    for n in range(n_slabs):
        u_a, u_gate = u_next
        if n + 1 < n_slabs:
            u_next = up(n + 1)
        a, gate = conv(u_a, n * slab), conv(u_gate, dff + n * slab)
        s = (gate * _sigmoid(gate) * a).astype(BF16)
        acc = acc + _dot(s, wdown_ref[n * slab:(n + 1) * slab, :])
    if final:
        acc = _rms(acc, finw_ref[...])
    out_ref[...] = acc


def _const_spec(shape):
    nd = len(shape)
    return pl.BlockSpec(shape, lambda *_: (0,) * nd, pipeline_mode=pl.Buffered(1))


def _params(sem):
    return pltpu.CompilerParams(dimension_semantics=sem, vmem_limit_bytes=VMEM_LIMIT)


def _head_out(B, T, heads, d, tm, dtype=F32):
    return (jax.ShapeDtypeStruct((B, heads, T, d), dtype),
            pl.BlockSpec((None, heads, tm, d), lambda b, i: (b, 0, i, 0)))


def _hg_proj(h, nw, lb_logits, w_in, slot, heads):
    B, T, D = h.shape
    tm = TM_PROJ
    dim = w_in.shape[1] // 4
    dk = dim // heads
    fs, hspec = _head_out(B, T, heads, dk, tm)
    bs, _ = _head_out(B, T, heads, dk, tm, BF16)
    tok_spec = pl.BlockSpec((None, tm, dim), lambda b, i: (b, i, 0))
    return pl.pallas_call(
        functools.partial(_hg_proj_kernel, slot, heads),
        grid=(B, T // tm),
        in_specs=[pl.BlockSpec((None, tm, D), lambda b, i: (b, i, 0)), _const_spec((1, D)),
                  _const_spec(lb_logits.shape), _const_spec(w_in.shape)],
        out_specs=[hspec, hspec, hspec, hspec, tok_spec],
        out_shape=[fs, fs, bs, fs, jax.ShapeDtypeStruct((B, T, dim), F32)],
        compiler_params=_params(("parallel", "parallel")),
        name="hg_proj",
    )(h, nw.reshape(1, D), lb_logits, w_in)


def _gla_proj(h, nw, w_main, w_lr, w_up, b_up, heads, kd, vd):
    B, T, D = h.shape
    tm = TM_PROJ
    fs, kspec = _head_out(B, T, heads, kd // heads, tm)
    vs, vspec = _head_out(B, T, heads, vd // heads, tm, BF16)
    tok_spec = pl.BlockSpec((None, tm, vd), lambda b, i: (b, i, 0))
    return pl.pallas_call(
        functools.partial(_gla_proj_kernel, heads, kd, vd),
        grid=(B, T // tm),
        in_specs=[pl.BlockSpec((None, tm, D), lambda b, i: (b, i, 0)), _const_spec((1, D)),
                  _const_spec(w_main.shape), _const_spec(w_lr.shape), _const_spec(w_up.shape),
                  _const_spec((1, kd))],
        out_specs=[kspec, kspec, vspec, kspec, tok_spec],
        out_shape=[fs, fs, vs, fs, jax.ShapeDtypeStruct((B, T, vd), F32)],
        compiler_params=_params(("parallel", "parallel")),
        name="gla_proj",
    )(h, nw.reshape(1, D), w_main, w_lr, w_up, b_up.reshape(1, kd))


def _attention(q, k, v, lf):
    B, H, T, dk = q.shape
    dv = v.shape[-1]
    rows = CHUNK * CHUNKS_PER_STEP
    kspec = pl.BlockSpec((None, H, rows, dk), lambda b, c: (b, 0, c, 0))
    vspec = pl.BlockSpec((None, H, rows, dv), lambda b, c: (b, 0, c, 0))
    return pl.pallas_call(
        functools.partial(_attn_kernel, H),
        grid=(B, T // rows),
        in_specs=[kspec, kspec, vspec, kspec],
        out_specs=vspec,
        out_shape=jax.ShapeDtypeStruct((B, H, T, dv), F32),
        scratch_shapes=[pltpu.VMEM((H, dv, dk), F32)],
        compiler_params=_params(("parallel", "arbitrary")),
        name="chunk_attn",
    )(q, k, v, lf)


def _dense(o, g, hnw, w_out, h, fnw, w_up, conv_w, conv_b, w_down, finw, final):
    B, H, T, dv = o.shape
    D = h.shape[-1]
    dff = w_down.shape[0]
    tm = TM_DENSE
    tok = lambda d: pl.BlockSpec((None, tm, d), lambda b, i: (b, i, 0))
    return pl.pallas_call(
        functools.partial(_dense_kernel, H, final),
        grid=(B, T // tm),
        in_specs=[pl.BlockSpec((None, H, tm, dv), lambda b, i: (b, 0, i, 0)), tok(H * dv),
                  _const_spec((1, dv)), _const_spec(w_out.shape), tok(D), _const_spec((1, D)),
                  _const_spec(w_up.shape), _const_spec(conv_w.shape), _const_spec((1, 2 * dff)),
                  _const_spec(w_down.shape), _const_spec((1, D))],
        out_specs=tok(D),
        out_shape=jax.ShapeDtypeStruct((B, T, D), F32),
        scratch_shapes=[pltpu.VMEM((SUBLANES, 2 * dff), F32)],
        compiler_params=_params(("parallel", "arbitrary")),
        name="dense_tail",
    )(o, g, hnw.reshape(1, dv), w_out, h, fnw.reshape(1, D), w_up, conv_w, conv_b.reshape(1, 2 * dff),
      w_down, finw.reshape(1, D))


def kernel(x, lb_logits, hg_w_in, hg_norm_w, hg_w_out, gla_w_in, gla_w_gk_up, gla_b_gk_up, gla_norm_w,
           gla_w_out, norm_mixer_w, norm_ffn_w, ffn_w_up, ffn_conv_w, ffn_conv_b, ffn_w_down, norm_final_w):
    depth = norm_mixer_w.shape[0]
    hg_heads = hg_w_out.shape[1] // hg_norm_w.shape[-1]
    gla_vd = gla_w_out.shape[1]
    gla_heads = gla_vd // gla_norm_w.shape[-1]
    gla_kd = gla_w_gk_up.shape[-1]
    rank = gla_w_gk_up.shape[1]
    bf = lambda w: w.astype(BF16)

    h = x
    for i in range(depth):
        j = i // 2
        if i % 2 == 0:
            q, k, v, lf, g = _hg_proj(h, norm_mixer_w[i], lb_logits, bf(hg_w_in[j]), i, hg_heads)
            hnw, w_out = hg_norm_w[j], hg_w_out[j]
        else:
            n_main = 2 * gla_kd + 2 * gla_vd
            w_in = gla_w_in[j]
            w_lr = jnp.pad(w_in[:, n_main:], ((0, 0), (0, LANES - rank)))
            w_up = jnp.pad(gla_w_gk_up[j], ((0, LANES - rank), (0, 0)))
            q, k, v, lf, g = _gla_proj(h, norm_mixer_w[i], bf(w_in[:, :n_main]), bf(w_lr), bf(w_up),
                                       gla_b_gk_up[j], gla_heads, gla_kd, gla_vd)
            hnw, w_out = gla_norm_w[j], gla_w_out[j]
        o = _attention(q, k, v, lf)
        h = _dense(o, g, hnw, bf(w_out), h, norm_ffn_w[i], bf(ffn_w_up[i]), ffn_conv_w[i], ffn_conv_b[i],
                   bf(ffn_w_down[i]), norm_final_w, i == depth - 1)
    return h
```

```python
import functools
import math

import jax
import jax.numpy as jnp
from jax import lax
from jax.experimental import pallas as pl
from jax.experimental.pallas import tpu as pltpu

EPS = 1e-6
GLA_GATE_NORMALIZER = 16.0
LANES = 128
SUBLANES = 8
VMEM_LIMIT = 56 * 1024 * 1024
LOG2E = math.log2(math.e)
CHUNK = 128
CHUNKS_PER_STEP = 4
LOCKSTEP = 8
TM_PROJ = 512
TM_DENSE = 512
DENSE_PARTS = 2

F32 = jnp.float32
BF16 = jnp.bfloat16


def _rms(x, w):
    return x * lax.rsqrt(jnp.mean(x * x, axis=-1, keepdims=True) + EPS) * w


def _sigmoid(x):
    return 1.0 / (1.0 + jnp.exp(-x))


def _dot(a, b):
    return jnp.dot(a, b, preferred_element_type=F32)


def _dot_nt(a, b):
    return lax.dot_general(a, b, (((1,), (1,)), ((), ())), preferred_element_type=F32)


def _dot_tn(a, b):
    return lax.dot_general(a, b, (((0,), (0,)), ((), ())), preferred_element_type=F32)


def _store_heads(ref, val, heads):
    d = val.shape[-1] // heads
    for h in range(heads):
        ref[h] = val[:, h * d:(h + 1) * d].astype(ref.dtype)


def _row_blocks(x):
    return [x[SUBLANES * j:SUBLANES * (j + 1)] for j in range(x.shape[0] // SUBLANES)]


def _bcast_row(blk, r):
    return jnp.broadcast_to(blk[r:r + 1], blk.shape)


def _hg_proj_kernel(slot, heads, x_ref, nw_ref, lbl_ref, w_ref, q_ref, k_ref, v_ref, lf_ref, g_ref):
    y = _rms(x_ref[...], nw_ref[...]).astype(BF16)
    dim = g_ref.shape[-1]
    dk = dim // heads
    sect = lambda s: _dot(y, w_ref[:, s * dim:(s + 1) * dim])
    l = lbl_ref[...]
    e = jnp.exp(l - jnp.max(l, axis=0, keepdims=True))
    lb = jnp.sum(e[:slot + 1], axis=0, keepdims=True) / jnp.sum(e, axis=0, keepdims=True)
    q = sect(0)
    _store_heads(q_ref, q * _sigmoid(q) * (dk ** -0.5), heads)
    forget = lb + (1.0 - lb) * _sigmoid(sect(1))
    _store_heads(k_ref, 1.0 - forget, heads)
    _store_heads(lf_ref, jnp.log(forget) * LOG2E, heads)
    _store_heads(v_ref, sect(2), heads)
    g_ref[...] = sect(3)


def _gla_proj_kernel(heads, kd, vd, x_ref, nw_ref, w_ref, wlr_ref, wup_ref, bup_ref,
                     q_ref, k_ref, v_ref, lf_ref, g_ref):
    y = _rms(x_ref[...], nw_ref[...]).astype(BF16)
    dk = kd // heads
    _store_heads(q_ref, _dot(y, w_ref[:, 0:kd]) * (dk ** -0.5), heads)
    _store_heads(k_ref, _dot(y, w_ref[:, kd:2 * kd]), heads)
    _store_heads(v_ref, _dot(y, w_ref[:, 2 * kd:2 * kd + vd]), heads)
    g_ref[...] = _dot(y, w_ref[:, 2 * kd + vd:2 * kd + 2 * vd])
    lr = _dot(y, wlr_ref[...]).astype(BF16)
    gk = _dot(lr, wup_ref[...]) + bup_ref[...]
    log_sig = jnp.minimum(gk, 0.0) - jnp.log(1.0 + jnp.exp(-jnp.abs(gk)))
    _store_heads(lf_ref, log_sig * (LOG2E / GLA_GATE_NORMALIZER), heads)


def _small_level(m, sub, cb, qb, kb):
    upper = (sub & m) != 0
    zs, cn = [], []
    for c, q, k in zip(cb, qb, kb):
        if m == 1:
            r = jnp.where(upper, pltpu.roll(c, 1, axis=0), c)
        elif m == 2:
            r = jnp.where(sub < 4, _bcast_row(c, 1), _bcast_row(c, 5))
        else:
            r = _bcast_row(c, 3)
        zs.append(jnp.where(upper, q, k) * jnp.exp2(jnp.where(upper, c, r - c)))
        cn.append(jnp.where(upper, c + r, c))
    return zs, cn


def _big_level(m, cb, qb, kb):
    nb = m // SUBLANES
    zs, cn = [], []
    for i in range(len(cb) // (2 * nb)):
        lo = range(2 * nb * i, 2 * nb * i + nb)
        r = _bcast_row(cb[lo[-1]], SUBLANES - 1)
        for j in lo:
            zs.append(kb[j] * jnp.exp2(r - cb[j]))
            cn.append(cb[j])
        for j in range(lo[-1] + 1, lo[-1] + 1 + nb):
            zs.append(qb[j] * jnp.exp2(cb[j]))
            cn.append(cb[j] + r)
    return zs, cn


def _attn_kernel(heads, q_ref, k_ref, v_ref, lf_ref, o_ref, st_ref):
    @pl.when(pl.program_id(1) == 0)
    def _():
        st_ref[...] = jnp.zeros_like(st_ref)

    C, dk = CHUNK, q_ref.shape[2]
    group = min(LOCKSTEP, heads)
    n_groups = heads // group
    span = max(1, LOCKSTEP // group)
    n_small = int(math.log2(SUBLANES))
    sub = lax.broadcasted_iota(jnp.int32, (SUBLANES, dk), 0)
    lane = lax.broadcasted_iota(jnp.int32, (SUBLANES, C), 1)
    ti = lax.broadcasted_iota(jnp.int32, (C, C), 0)
    si = lax.broadcasted_iota(jnp.int32, (C, C), 1)
    x = ti ^ si
    small_id = jnp.full((C, C), -1, jnp.int32)
    for l in reversed(range(n_small)):
        small_id = jnp.where(x < 2 ** (l + 1), l + 1, small_id)
    small_id = jnp.where(x == 0, 0, jnp.where((ti < si) | (x >= SUBLANES), -1, small_id))
    small_id_b = _row_blocks(small_id)

    def chunk_head_group(it, carry):
        first = (it // n_groups) * span
        members = [(pl.ds(pl.multiple_of((first + cc) * C, C), C), (it % n_groups) * group + i)
                   for cc in range(span) for i in range(group)]
        n = len(members)
        qs = [q_ref[h, rows, :] for rows, h in members]
        ks = [k_ref[h, rows, :] for rows, h in members]
        qb = [_row_blocks(q) for q in qs]
        kb = [_row_blocks(k) for k in ks]
        cb = [_row_blocks(lf_ref[h, rows, :]) for rows, h in members]
        ab = [None] * n

        def scores(zs):
            z = jnp.concatenate(zs, axis=0).astype(BF16)
            return _dot_nt(z, z)

        ps = [[_dot_nt(qs[i].astype(BF16), ks[i].astype(BF16))] for i in range(n)]
        for l in range(n_small):
            for i in range(n):
                zs, cb[i] = _small_level(2 ** l, sub, cb[i], qb[i], kb[i])
                ps[i].append(scores(zs))
        for i in range(n):
            pb = [_row_blocks(p) for p in ps[i]]
            blocks = []
            for j, sid in enumerate(small_id_b):
                a = jnp.zeros_like(pb[0][j])
                for l in reversed(range(n_small + 1)):
                    a = jnp.where(sid == l, pb[l][j], a)
                blocks.append(a)
            ab[i] = blocks
        m = SUBLANES
        while m < C:
            for i in range(n):
                zs, cb[i] = _big_level(m, cb[i], qb[i], kb[i])
                pb = _row_blocks(scores(zs))
                nb = m // SUBLANES
                for blk in range(C // (2 * m)):
                    keep = (lane >= 2 * m * blk) & (lane < 2 * m * blk + m)
                    for j in range(2 * nb * blk + nb, 2 * nb * (blk + 1)):
                        ab[i][j] = jnp.where(keep, pb[j], ab[i][j])
            m *= 2
        for i, (rows, h) in enumerate(members):
            c = jnp.concatenate(cb[i], axis=0)
            a = jnp.concatenate(ab[i], axis=0).astype(BF16)
            v = v_ref[h, rows, :]
            st = st_ref[h]
            o_ref[h, rows, :] = _dot(a, v) + _dot_nt((qs[i] * jnp.exp2(c)).astype(BF16), st.astype(BF16))
            c_last = c[C - 1:C, :]
            kd = (ks[i] * jnp.exp2(c_last - c)).astype(BF16)
            st_ref[h] = st * jnp.exp2(c_last) + _dot_tn(v, kd)
        return carry

    lax.fori_loop(0, (q_ref.shape[1] // (C * span)) * n_groups, chunk_head_group, 0)


def _dense_kernel(heads, final, o_ref, g_ref, hnw_ref, wout_ref, h_ref, fnw_ref, wup_ref, cw_ref,
                  cb_ref, wdown_ref, finw_ref, out_ref, carry_ref):
    dff = wdown_ref.shape[0]

    @pl.when(pl.program_id(1) == 0)
    def _():
        carry_ref[...] = jnp.zeros_like(carry_ref)

    tm = h_ref.shape[0]
    rows_per = tm // DENSE_PARTS
    slab = 2 * LANES
    n_slabs = dff // slab
    hw = hnw_ref[...]

    def mixer_out(r0):
        rows = slice(r0, r0 + rows_per)
        parts = []
        for hh in range(heads):
            o = o_ref[hh, rows, :]
            parts.append(o * lax.rsqrt(jnp.mean(o * o, axis=-1, keepdims=True) + EPS) * hw)
        g = g_ref[rows, :]
        mix_in = (jnp.concatenate(parts, axis=-1) * (g * _sigmoid(g))).astype(BF16)
        h = h_ref[rows, :] + _dot(mix_in, wout_ref[...])
        return h, _rms(h, fnw_ref[...]).astype(BF16)

    up = lambda y, n: [_dot(y, wup_ref[:, base:base + slab]) for base in (n * slab, dff + n * slab)]

    sub = lax.broadcasted_iota(jnp.int32, (SUBLANES, slab), 0)

    def delayed(blocks):
        rot = [pltpu.roll(b, 1, axis=0) for b in blocks]
        return [rot[0]] + [jnp.where(sub < 1, rot[i - 1], rot[i]) for i in range(1, len(rot))]

    def conv(u, base):
        blocks = [carry_ref[:, base:base + slab]] + _row_blocks(u)
        carry_ref[:, base:base + slab] = blocks[-1]
        cw = cw_ref[:, base:base + slab]
        w0, w1, w2, b = cw[0:1], cw[1:2], cw[2:3], cb_ref[:, base:base + slab]
        inner = [w1 * x + d for x, d in zip(blocks, delayed([w0 * x for x in blocks]))]
        return jnp.concatenate([b + w2 * x + d for x, d in zip(blocks[1:], delayed(inner)[1:])], axis=0)

    parts_hy = [mixer_out(0)]
    u_next = up(parts_hy[0][1], 0)
    for p in range(DENSE_PARTS):
        acc, y = parts_hy[p]
        if p + 1 < DENSE_PARTS:
            parts_hy.append(mixer_out((p + 1) * rows_per))
        for n in range(n_slabs):
            u_a, u_gate = u_next
            if n + 1 < n_slabs:
                u_next = up(y, n + 1)
            elif p + 1 < DENSE_PARTS:
                u_next = up(parts_hy[p + 1][1], 0)
            a, gate = conv(u_a, n * slab), conv(u_gate, dff + n * slab)
            s = (gate * _sigmoid(gate) * a).astype(BF16)
            acc = acc + _dot(s, wdown_ref[n * slab:(n + 1) * slab, :])
        if final:
            acc = _rms(acc, finw_ref[...])
        out_ref[p * rows_per:(p + 1) * rows_per, :] = acc


def _const_spec(shape):
    nd = len(shape)
    return pl.BlockSpec(shape, lambda *_: (0,) * nd, pipeline_mode=pl.Buffered(1))


def _params(sem):
    return pltpu.CompilerParams(dimension_semantics=sem, vmem_limit_bytes=VMEM_LIMIT)


def _head_out(B, T, heads, d, tm, dtype=F32):
    return (jax.ShapeDtypeStruct((B, heads, T, d), dtype),
            pl.BlockSpec((None, heads, tm, d), lambda b, i: (b, 0, i, 0)))


def _hg_proj(h, nw, lb_logits, w_in, slot, heads):
    B, T, D = h.shape
    tm = TM_PROJ
    dim = w_in.shape[1] // 4
    dk = dim // heads
    fs, hspec = _head_out(B, T, heads, dk, tm)
    bs, _ = _head_out(B, T, heads, dk, tm, BF16)
    tok_spec = pl.BlockSpec((None, tm, dim), lambda b, i: (b, i, 0))
    return pl.pallas_call(
        functools.partial(_hg_proj_kernel, slot, heads),
        grid=(B, T // tm),
        in_specs=[pl.BlockSpec((None, tm, D), lambda b, i: (b, i, 0)), _const_spec((1, D)),
                  _const_spec(lb_logits.shape), _const_spec(w_in.shape)],
        out_specs=[hspec, hspec, hspec, hspec, tok_spec],
        out_shape=[fs, fs, bs, fs, jax.ShapeDtypeStruct((B, T, dim), F32)],
        compiler_params=_params(("parallel", "parallel")),
        name="hg_proj",
    )(h, nw.reshape(1, D), lb_logits, w_in)


def _gla_proj(h, nw, w_main, w_lr, w_up, b_up, heads, kd, vd):
    B, T, D = h.shape
    tm = TM_PROJ
    fs, kspec = _head_out(B, T, heads, kd // heads, tm)
    vs, vspec = _head_out(B, T, heads, vd // heads, tm, BF16)
    tok_spec = pl.BlockSpec((None, tm, vd), lambda b, i: (b, i, 0))
    return pl.pallas_call(
        functools.partial(_gla_proj_kernel, heads, kd, vd),
        grid=(B, T // tm),
        in_specs=[pl.BlockSpec((None, tm, D), lambda b, i: (b, i, 0)), _const_spec((1, D)),
                  _const_spec(w_main.shape), _const_spec(w_lr.shape), _const_spec(w_up.shape),
                  _const_spec((1, kd))],
        out_specs=[kspec, kspec, vspec, kspec, tok_spec],
        out_shape=[fs, fs, vs, fs, jax.ShapeDtypeStruct((B, T, vd), F32)],
        compiler_params=_params(("parallel", "parallel")),
        name="gla_proj",
    )(h, nw.reshape(1, D), w_main, w_lr, w_up, b_up.reshape(1, kd))


def _attention(q, k, v, lf):
    B, H, T, dk = q.shape
    dv = v.shape[-1]
    rows = CHUNK * CHUNKS_PER_STEP
    kspec = pl.BlockSpec((None, H, rows, dk), lambda b, c: (b, 0, c, 0))
    vspec = pl.BlockSpec((None, H, rows, dv), lambda b, c: (b, 0, c, 0))
    return pl.pallas_call(
        functools.partial(_attn_kernel, H),
        grid=(B, T // rows),
        in_specs=[kspec, kspec, vspec, kspec],
        out_specs=vspec,
        out_shape=jax.ShapeDtypeStruct((B, H, T, dv), F32),
        scratch_shapes=[pltpu.VMEM((H, dv, dk), F32)],
        compiler_params=_params(("parallel", "arbitrary")),
        name="chunk_attn",
    )(q, k, v, lf)


def _dense(o, g, hnw, w_out, h, fnw, w_up, conv_w, conv_b, w_down, finw, final):
    B, H, T, dv = o.shape
    D = h.shape[-1]
    dff = w_down.shape[0]
    tm = TM_DENSE
    tok = lambda d: pl.BlockSpec((None, tm, d), lambda b, i: (b, i, 0))
    return pl.pallas_call(
        functools.partial(_dense_kernel, H, final),
        grid=(B, T // tm),
        in_specs=[pl.BlockSpec((None, H, tm, dv), lambda b, i: (b, 0, i, 0)), tok(H * dv),
                  _const_spec((1, dv)), _const_spec(w_out.shape), tok(D), _const_spec((1, D)),
                  _const_spec(w_up.shape), _const_spec(conv_w.shape), _const_spec((1, 2 * dff)),
                  _const_spec(w_down.shape), _const_spec((1, D))],
        out_specs=tok(D),
        out_shape=jax.ShapeDtypeStruct((B, T, D), F32),
        scratch_shapes=[pltpu.VMEM((SUBLANES, 2 * dff), F32)],
        compiler_params=_params(("parallel", "arbitrary")),
        name="dense_tail",
    )(o, g, hnw.reshape(1, dv), w_out, h, fnw.reshape(1, D), w_up, conv_w, conv_b.reshape(1, 2 * dff),
      w_down, finw.reshape(1, D))


def kernel(x, lb_logits, hg_w_in, hg_norm_w, hg_w_out, gla_w_in, gla_w_gk_up, gla_b_gk_up, gla_norm_w,
           gla_w_out, norm_mixer_w, norm_ffn_w, ffn_w_up, ffn_conv_w, ffn_conv_b, ffn_w_down, norm_final_w):
    depth = norm_mixer_w.shape[0]
    hg_heads = hg_w_out.shape[1] // hg_norm_w.shape[-1]
    gla_vd = gla_w_out.shape[1]
    gla_heads = gla_vd // gla_norm_w.shape[-1]
    gla_kd = gla_w_gk_up.shape[-1]
    rank = gla_w_gk_up.shape[1]
    bf = lambda w: w.astype(BF16)

    h = x
    for i in range(depth):
        j = i // 2
        if i % 2 == 0:
            q, k, v, lf, g = _hg_proj(h, norm_mixer_w[i], lb_logits, bf(hg_w_in[j]), i, hg_heads)
            hnw, w_out = hg_norm_w[j], hg_w_out[j]
        else:
            n_main = 2 * gla_kd + 2 * gla_vd
            w_in = gla_w_in[j]
            w_lr = jnp.pad(w_in[:, n_main:], ((0, 0), (0, LANES - rank)))
            w_up = jnp.pad(gla_w_gk_up[j], ((0, LANES - rank), (0, 0)))
            q, k, v, lf, g = _gla_proj(h, norm_mixer_w[i], bf(w_in[:, :n_main]), bf(w_lr), bf(w_up),
                                       gla_b_gk_up[j], gla_heads, gla_kd, gla_vd)
            hnw, w_out = gla_norm_w[j], gla_w_out[j]
        o = _attention(q, k, v, lf)
        h = _dense(o, g, hnw, bf(w_out), h, norm_ffn_w[i], bf(ffn_w_up[i]), ffn_conv_w[i], ffn_conv_b[i],
                   bf(ffn_w_down[i]), norm_final_w, i == depth - 1)
    return h
```

```python
import functools
import math

import jax
import jax.numpy as jnp
from jax import lax
from jax.experimental import pallas as pl
from jax.experimental.pallas import tpu as pltpu

EPS = 1e-6
GLA_GATE_NORMALIZER = 16.0
LANES = 128
SUBLANES = 8
VMEM_LIMIT = 56 * 1024 * 1024
LOG2E = math.log2(math.e)
CHUNK = 128
CHUNKS_PER_STEP = 4
LOCKSTEP = 8
TM_PROJ = 1024
TM_DENSE = 512
DENSE_PARTS = 2
UP_AHEAD = 1

F32 = jnp.float32
BF16 = jnp.bfloat16


def _rms(x, w):
    return x * lax.rsqrt(jnp.mean(x * x, axis=-1, keepdims=True) + EPS) * w


def _sigmoid(x):
    return 1.0 / (1.0 + jnp.exp(-x))


def _dot(a, b):
    return jnp.dot(a, b, preferred_element_type=F32)


def _dot_nt(a, b):
    return lax.dot_general(a, b, (((1,), (1,)), ((), ())), preferred_element_type=F32)


def _dot_tn(a, b):
    return lax.dot_general(a, b, (((0,), (0,)), ((), ())), preferred_element_type=F32)


def _store_heads(ref, val, heads):
    d = val.shape[-1] // heads
    for h in range(heads):
        ref[h] = val[:, h * d:(h + 1) * d].astype(ref.dtype)


def _row_blocks(x):
    return [x[SUBLANES * j:SUBLANES * (j + 1)] for j in range(x.shape[0] // SUBLANES)]


def _bcast_row(blk, r):
    return jnp.broadcast_to(blk[r:r + 1], blk.shape)


def _hg_proj_kernel(slot, heads, x_ref, nw_ref, lbl_ref, w_ref, q_ref, k_ref, v_ref, lf_ref, g_ref):
    y = _rms(x_ref[...], nw_ref[...]).astype(BF16)
    dim = g_ref.shape[-1]
    dk = dim // heads
    sect = lambda s: _dot(y, w_ref[:, s * dim:(s + 1) * dim])
    l = lbl_ref[...]
    e = jnp.exp(l - jnp.max(l, axis=0, keepdims=True))
    lb = jnp.sum(e[:slot + 1], axis=0, keepdims=True) / jnp.sum(e, axis=0, keepdims=True)
    q = sect(0)
    _store_heads(q_ref, q * _sigmoid(q) * (dk ** -0.5), heads)
    forget = lb + (1.0 - lb) * _sigmoid(sect(1))
    _store_heads(k_ref, 1.0 - forget, heads)
    _store_heads(lf_ref, jnp.log(forget) * LOG2E, heads)
    _store_heads(v_ref, sect(2), heads)
    g_ref[...] = sect(3).astype(g_ref.dtype)


def _gla_proj_kernel(heads, kd, vd, x_ref, nw_ref, w_ref, wlr_ref, wup_ref, bup_ref,
                     q_ref, k_ref, v_ref, lf_ref, g_ref):
    y = _rms(x_ref[...], nw_ref[...]).astype(BF16)
    dk = kd // heads
    _store_heads(q_ref, _dot(y, w_ref[:, 0:kd]) * (dk ** -0.5), heads)
    _store_heads(k_ref, _dot(y, w_ref[:, kd:2 * kd]), heads)
    _store_heads(v_ref, _dot(y, w_ref[:, 2 * kd:2 * kd + vd]), heads)
    g_ref[...] = _dot(y, w_ref[:, 2 * kd + vd:2 * kd + 2 * vd]).astype(g_ref.dtype)
    lr = _dot(y, wlr_ref[...]).astype(BF16)
    gk = _dot(lr, wup_ref[...]) + bup_ref[...]
    log_sig = jnp.minimum(gk, 0.0) - jnp.log(1.0 + jnp.exp(-jnp.abs(gk)))
    _store_heads(lf_ref, log_sig * (LOG2E / GLA_GATE_NORMALIZER), heads)


def _small_level(m, sub, cb, qb, kb):
    upper = (sub & m) != 0
    zs, cn = [], []
    for c, q, k in zip(cb, qb, kb):
        if m == 1:
            r = jnp.where(upper, pltpu.roll(c, 1, axis=0), c)
        elif m == 2:
            r = jnp.where(sub < 4, _bcast_row(c, 1), _bcast_row(c, 5))
        else:
            r = _bcast_row(c, 3)
        zs.append(jnp.where(upper, q, k) * jnp.exp2(jnp.where(upper, c, r - c)))
        cn.append(jnp.where(upper, c + r, c))
    return zs, cn


def _big_level(m, cb, qb, kb):
    nb = m // SUBLANES
    zs, cn = [], []
    for i in range(len(cb) // (2 * nb)):
        lo = range(2 * nb * i, 2 * nb * i + nb)
        r = _bcast_row(cb[lo[-1]], SUBLANES - 1)
        for j in lo:
            zs.append(kb[j] * jnp.exp2(r - cb[j]))
            cn.append(cb[j])
        for j in range(lo[-1] + 1, lo[-1] + 1 + nb):
            zs.append(qb[j] * jnp.exp2(cb[j]))
            cn.append(cb[j] + r)
    return zs, cn


def _attn_kernel(heads, q_ref, k_ref, v_ref, lf_ref, o_ref, st_ref):
    @pl.when(pl.program_id(1) == 0)
    def _():
        st_ref[...] = jnp.zeros_like(st_ref)

    C, dk = CHUNK, q_ref.shape[2]
    group = min(LOCKSTEP, heads)
    n_groups = heads // group
    span = max(1, LOCKSTEP // group)
    n_small = int(math.log2(SUBLANES))
    sub = lax.broadcasted_iota(jnp.int32, (SUBLANES, dk), 0)
    lane = lax.broadcasted_iota(jnp.int32, (SUBLANES, C), 1)
    ti = lax.broadcasted_iota(jnp.int32, (C, C), 0)
    si = lax.broadcasted_iota(jnp.int32, (C, C), 1)
    x = ti ^ si
    small_id = jnp.full((C, C), -1, jnp.int32)
    for l in reversed(range(n_small)):
        small_id = jnp.where(x < 2 ** (l + 1), l + 1, small_id)
    small_id = jnp.where(x == 0, 0, jnp.where((ti < si) | (x >= SUBLANES), -1, small_id))
    small_id_b = _row_blocks(small_id)

    def chunk_head_group(it, carry):
        first = (it // n_groups) * span
        members = [(pl.ds(pl.multiple_of((first + cc) * C, C), C), (it % n_groups) * group + i)
                   for cc in range(span) for i in range(group)]
        n = len(members)
        qs = [q_ref[h, rows, :] for rows, h in members]
        ks = [k_ref[h, rows, :] for rows, h in members]
        qb = [_row_blocks(q) for q in qs]
        kb = [_row_blocks(k) for k in ks]
        cb = [_row_blocks(lf_ref[h, rows, :]) for rows, h in members]
        ab = [None] * n

        def scores(zs):
            z = jnp.concatenate(zs, axis=0).astype(BF16)
            return _dot_nt(z, z)

        ps = [[_dot_nt(qs[i].astype(BF16), ks[i].astype(BF16))] for i in range(n)]
        for l in range(n_small):
            for i in range(n):
                zs, cb[i] = _small_level(2 ** l, sub, cb[i], qb[i], kb[i])
                ps[i].append(scores(zs))
        for i in range(n):
            pb = [_row_blocks(p) for p in ps[i]]
            blocks = []
            for j, sid in enumerate(small_id_b):
                a = jnp.zeros_like(pb[0][j])
                for l in reversed(range(n_small + 1)):
                    a = jnp.where(sid == l, pb[l][j], a)
                blocks.append(a)
            ab[i] = blocks
        m = SUBLANES
        while m < C:
            for i in range(n):
                zs, cb[i] = _big_level(m, cb[i], qb[i], kb[i])
                pb = _row_blocks(scores(zs))
                nb = m // SUBLANES
                for blk in range(C // (2 * m)):
                    keep = (lane >= 2 * m * blk) & (lane < 2 * m * blk + m)
                    for j in range(2 * nb * blk + nb, 2 * nb * (blk + 1)):
                        ab[i][j] = jnp.where(keep, pb[j], ab[i][j])
            m *= 2
        for i, (rows, h) in enumerate(members):
            c = jnp.concatenate(cb[i], axis=0)
            a = jnp.concatenate(ab[i], axis=0).astype(BF16)
            v = v_ref[h, rows, :]
            st = st_ref[h]
            o = _dot(a, v) + _dot_nt((qs[i] * jnp.exp2(c)).astype(BF16), st.astype(BF16))
            o_ref[h, rows, :] = o.astype(o_ref.dtype)
            c_last = c[C - 1:C, :]
            kd = (ks[i] * jnp.exp2(c_last - c)).astype(BF16)
            st_ref[h] = st * jnp.exp2(c_last) + _dot_tn(v, kd)
        return carry

    lax.fori_loop(0, (q_ref.shape[1] // (C * span)) * n_groups, chunk_head_group, 0)


def _dense_kernel(heads, final, o_ref, g_ref, hnw_ref, wout_ref, h_ref, fnw_ref, wup_ref, cw_ref,
                  cb_ref, wdown_ref, finw_ref, out_ref, carry_ref):
    dff = wdown_ref.shape[0]

    @pl.when(pl.program_id(1) == 0)
    def _():
        carry_ref[...] = jnp.zeros_like(carry_ref)

    tm = h_ref.shape[0]
    rows_per = tm // DENSE_PARTS
    slab = 2 * LANES
    n_slabs = dff // slab
    hw = hnw_ref[...]

    def mixer_out(r0):
        rows = slice(r0, r0 + rows_per)
        parts = []
        for hh in range(heads):
            o = o_ref[hh, rows, :].astype(F32)
            parts.append(o * lax.rsqrt(jnp.mean(o * o, axis=-1, keepdims=True) + EPS) * hw)
        g = g_ref[rows, :].astype(F32)
        mix_in = (jnp.concatenate(parts, axis=-1) * (g * _sigmoid(g))).astype(BF16)
        h = h_ref[rows, :] + _dot(mix_in, wout_ref[...])
        return h, _rms(h, fnw_ref[...]).astype(BF16)

    up = lambda y, n: [_dot(y, wup_ref[:, base:base + slab]) for base in (n * slab, dff + n * slab)]

    sub = lax.broadcasted_iota(jnp.int32, (SUBLANES, slab), 0)

    def delayed(blocks):
        rot = [pltpu.roll(b, 1, axis=0) for b in blocks]
        return [rot[0]] + [jnp.where(sub < 1, rot[i - 1], rot[i]) for i in range(1, len(rot))]

    def conv(u, base):
        blocks = [carry_ref[:, base:base + slab]] + _row_blocks(u)
        carry_ref[:, base:base + slab] = blocks[-1]
        cw = cw_ref[:, base:base + slab]
        w0, w1, w2, b = cw[0:1], cw[1:2], cw[2:3], cb_ref[:, base:base + slab]
        inner = [w1 * x + d for x, d in zip(blocks, delayed([w0 * x for x in blocks]))]
        return jnp.concatenate([b + w2 * x + d for x, d in zip(blocks[1:], delayed(inner)[1:])], axis=0)

    order = [(p, n) for p in range(DENSE_PARTS) for n in range(n_slabs)]
    parts_hy = [mixer_out(0)]
    ups = {i: up(parts_hy[0][1], order[i][1]) for i in range(UP_AHEAD)}
    for p in range(1, DENSE_PARTS):
        parts_hy.append(mixer_out(p * rows_per))
    for i, (p, n) in enumerate(order):
        if n == 0:
            acc = parts_hy[p][0]
        u_a, u_gate = ups.pop(i)
        if i + UP_AHEAD < len(order):
            pa, na = order[i + UP_AHEAD]
            ups[i + UP_AHEAD] = up(parts_hy[pa][1], na)
        a, gate = conv(u_a, n * slab), conv(u_gate, dff + n * slab)
        s = (gate * _sigmoid(gate) * a).astype(BF16)
        acc = acc + _dot(s, wdown_ref[n * slab:(n + 1) * slab, :])
        if n + 1 == n_slabs:
            if final:
                acc = _rms(acc, finw_ref[...])
            out_ref[p * rows_per:(p + 1) * rows_per, :] = acc


def _const_spec(shape):
    nd = len(shape)
    return pl.BlockSpec(shape, lambda *_: (0,) * nd, pipeline_mode=pl.Buffered(1))


def _params(sem):
    return pltpu.CompilerParams(dimension_semantics=sem, vmem_limit_bytes=VMEM_LIMIT)


def _head_out(B, T, heads, d, tm, dtype=F32):
    return (jax.ShapeDtypeStruct((B, heads, T, d), dtype),
            pl.BlockSpec((None, heads, tm, d), lambda b, i: (b, 0, i, 0)))


def _hg_proj(h, nw, lb_logits, w_in, slot, heads):
    B, T, D = h.shape
    tm = TM_PROJ
    dim = w_in.shape[1] // 4
    dk = dim // heads
    fs, hspec = _head_out(B, T, heads, dk, tm)
    bs, _ = _head_out(B, T, heads, dk, tm, BF16)
    tok_spec = pl.BlockSpec((None, tm, dim), lambda b, i: (b, i, 0))
    return pl.pallas_call(
        functools.partial(_hg_proj_kernel, slot, heads),
        grid=(B, T // tm),
        in_specs=[pl.BlockSpec((None, tm, D), lambda b, i: (b, i, 0)), _const_spec((1, D)),
                  _const_spec(lb_logits.shape), _const_spec(w_in.shape)],
        out_specs=[hspec, hspec, hspec, hspec, tok_spec],
        out_shape=[fs, fs, bs, fs, jax.ShapeDtypeStruct((B, T, dim), BF16)],
        compiler_params=_params(("parallel", "parallel")),
        name="hg_proj",
    )(h, nw.reshape(1, D), lb_logits, w_in)


def _gla_proj(h, nw, w_main, w_lr, w_up, b_up, heads, kd, vd):
    B, T, D = h.shape
    tm = TM_PROJ
    fs, kspec = _head_out(B, T, heads, kd // heads, tm)
    vs, vspec = _head_out(B, T, heads, vd // heads, tm, BF16)
    tok_spec = pl.BlockSpec((None, tm, vd), lambda b, i: (b, i, 0))
    return pl.pallas_call(
        functools.partial(_gla_proj_kernel, heads, kd, vd),
        grid=(B, T // tm),
        in_specs=[pl.BlockSpec((None, tm, D), lambda b, i: (b, i, 0)), _const_spec((1, D)),
                  _const_spec(w_main.shape), _const_spec(w_lr.shape), _const_spec(w_up.shape),
                  _const_spec((1, kd))],
        out_specs=[kspec, kspec, vspec, kspec, tok_spec],
        out_shape=[fs, fs, vs, fs, jax.ShapeDtypeStruct((B, T, vd), BF16)],
        compiler_params=_params(("parallel", "parallel")),
        name="gla_proj",
    )(h, nw.reshape(1, D), w_main, w_lr, w_up, b_up.reshape(1, kd))


def _attention(q, k, v, lf):
    B, H, T, dk = q.shape
    dv = v.shape[-1]
    rows = CHUNK * CHUNKS_PER_STEP
    kspec = pl.BlockSpec((None, H, rows, dk), lambda b, c: (b, 0, c, 0))
    vspec = pl.BlockSpec((None, H, rows, dv), lambda b, c: (b, 0, c, 0))
    return pl.pallas_call(
        functools.partial(_attn_kernel, H),
        grid=(B, T // rows),
        in_specs=[kspec, kspec, vspec, kspec],
        out_specs=vspec,
        out_shape=jax.ShapeDtypeStruct((B, H, T, dv), BF16),
        scratch_shapes=[pltpu.VMEM((H, dv, dk), F32)],
        compiler_params=_params(("parallel", "arbitrary")),
        name="chunk_attn",
    )(q, k, v, lf)


def _dense(o, g, hnw, w_out, h, fnw, w_up, conv_w, conv_b, w_down, finw, final):
    B, H, T, dv = o.shape
    D = h.shape[-1]
    dff = w_down.shape[0]
    tm = TM_DENSE
    tok = lambda d: pl.BlockSpec((None, tm, d), lambda b, i: (b, i, 0))
    return pl.pallas_call(
        functools.partial(_dense_kernel, H, final),
        grid=(B, T // tm),
        in_specs=[pl.BlockSpec((None, H, tm, dv), lambda b, i: (b, 0, i, 0)), tok(H * dv),
                  _const_spec((1, dv)), _const_spec(w_out.shape), tok(D), _const_spec((1, D)),
                  _const_spec(w_up.shape), _const_spec(conv_w.shape), _const_spec((1, 2 * dff)),
                  _const_spec(w_down.shape), _const_spec((1, D))],
        out_specs=tok(D),
        out_shape=jax.ShapeDtypeStruct((B, T, D), F32),
        scratch_shapes=[pltpu.VMEM((SUBLANES, 2 * dff), F32)],
        compiler_params=_params(("parallel", "arbitrary")),
        name="dense_tail",
    )(o, g, hnw.reshape(1, dv), w_out, h, fnw.reshape(1, D), w_up, conv_w, conv_b.reshape(1, 2 * dff),
      w_down, finw.reshape(1, D))


def kernel(x, lb_logits, hg_w_in, hg_norm_w, hg_w_out, gla_w_in, gla_w_gk_up, gla_b_gk_up, gla_norm_w,
           gla_w_out, norm_mixer_w, norm_ffn_w, ffn_w_up, ffn_conv_w, ffn_conv_b, ffn_w_down, norm_final_w):
    depth = norm_mixer_w.shape[0]
    hg_heads = hg_w_out.shape[1] // hg_norm_w.shape[-1]
    gla_vd = gla_w_out.shape[1]
    gla_heads = gla_vd // gla_norm_w.shape[-1]
    gla_kd = gla_w_gk_up.shape[-1]
    rank = gla_w_gk_up.shape[1]
    bf = lambda w: w.astype(BF16)

    h = x
    for i in range(depth):
        j = i // 2
        if i % 2 == 0:
            q, k, v, lf, g = _hg_proj(h, norm_mixer_w[i], lb_logits, bf(hg_w_in[j]), i, hg_heads)
            hnw, w_out = hg_norm_w[j], hg_w_out[j]
        else:
            n_main = 2 * gla_kd + 2 * gla_vd
            w_in = gla_w_in[j]
            w_lr = jnp.pad(w_in[:, n_main:], ((0, 0), (0, LANES - rank)))
            w_up = jnp.pad(gla_w_gk_up[j], ((0, LANES - rank), (0, 0)))
            q, k, v, lf, g = _gla_proj(h, norm_mixer_w[i], bf(w_in[:, :n_main]), bf(w_lr), bf(w_up),
                                       gla_b_gk_up[j], gla_heads, gla_kd, gla_vd)
            hnw, w_out = gla_norm_w[j], gla_w_out[j]
        o = _attention(q, k, v, lf)
        h = _dense(o, g, hnw, bf(w_out), h, norm_ffn_w[i], bf(ffn_w_up[i]), ffn_conv_w[i], ffn_conv_b[i],
                   bf(ffn_w_down[i]), norm_final_w, i == depth - 1)
    return h
```

```python
import functools
import math

import jax
import jax.numpy as jnp
from jax import lax
from jax.experimental import pallas as pl
from jax.experimental.pallas import tpu as pltpu

EPS = 1e-6
GLA_GATE_NORMALIZER = 16.0
LANES = 128
SUBLANES = 8
VMEM_LIMIT = 56 * 1024 * 1024
LOG2E = math.log2(math.e)
CHUNK = 128
CHUNKS_PER_STEP = 4
LOCKSTEP = 4
TM_PROJ = 1024
TM_DENSE = 512
DENSE_PARTS = 2
UP_AHEAD = 1

F32 = jnp.float32
BF16 = jnp.bfloat16


def _rms(x, w):
    return x * lax.rsqrt(jnp.mean(x * x, axis=-1, keepdims=True) + EPS) * w


def _sigmoid(x):
    return 1.0 / (1.0 + jnp.exp(-x))


def _dot(a, b):
    return jnp.dot(a, b, preferred_element_type=F32)


def _dot_nt(a, b):
    return lax.dot_general(a, b, (((1,), (1,)), ((), ())), preferred_element_type=F32)


def _dot_tn(a, b):
    return lax.dot_general(a, b, (((0,), (0,)), ((), ())), preferred_element_type=F32)


def _store_heads(ref, val, heads):
    d = val.shape[-1] // heads
    for h in range(heads):
        ref[h] = val[:, h * d:(h + 1) * d].astype(ref.dtype)


def _row_blocks(x):
    return [x[SUBLANES * j:SUBLANES * (j + 1)] for j in range(x.shape[0] // SUBLANES)]


def _bcast_row(blk, r):
    return jnp.broadcast_to(blk[r:r + 1], blk.shape)


def _hg_proj_kernel(slot, heads, x_ref, nw_ref, lbl_ref, w_ref, q_ref, k_ref, v_ref, lf_ref, g_ref):
    y = _rms(x_ref[...], nw_ref[...]).astype(BF16)
    dim = g_ref.shape[-1]
    dk = dim // heads
    sect = lambda s: _dot(y, w_ref[:, s * dim:(s + 1) * dim])
    l = lbl_ref[...]
    e = jnp.exp(l - jnp.max(l, axis=0, keepdims=True))
    lb = jnp.sum(e[:slot + 1], axis=0, keepdims=True) / jnp.sum(e, axis=0, keepdims=True)
    q = sect(0)
    _store_heads(q_ref, q * _sigmoid(q) * (dk ** -0.5), heads)
    forget = lb + (1.0 - lb) * _sigmoid(sect(1))
    _store_heads(k_ref, 1.0 - forget, heads)
    _store_heads(lf_ref, jnp.log(forget) * LOG2E, heads)
    _store_heads(v_ref, sect(2), heads)
    g_ref[...] = sect(3).astype(g_ref.dtype)


def _gla_proj_kernel(heads, kd, vd, x_ref, nw_ref, w_ref, wlr_ref, wup_ref, bup_ref,
                     q_ref, k_ref, v_ref, lf_ref, g_ref):
    y = _rms(x_ref[...], nw_ref[...]).astype(BF16)
    dk = kd // heads
    _store_heads(q_ref, _dot(y, w_ref[:, 0:kd]) * (dk ** -0.5), heads)
    _store_heads(k_ref, _dot(y, w_ref[:, kd:2 * kd]), heads)
    _store_heads(v_ref, _dot(y, w_ref[:, 2 * kd:2 * kd + vd]), heads)
    g_ref[...] = _dot(y, w_ref[:, 2 * kd + vd:2 * kd + 2 * vd]).astype(g_ref.dtype)
    lr = _dot(y, wlr_ref[...]).astype(BF16)
    gk = _dot(lr, wup_ref[...]) + bup_ref[...]
    log_sig = jnp.minimum(gk, 0.0) - jnp.log(1.0 + jnp.exp(-jnp.abs(gk)))
    _store_heads(lf_ref, log_sig * (LOG2E / GLA_GATE_NORMALIZER), heads)


def _small_level(m, sub, cb, qb, kb):
    upper = (sub & m) != 0
    zs, cn = [], []
    for c, q, k in zip(cb, qb, kb):
        if m == 1:
            r = jnp.where(upper, pltpu.roll(c, 1, axis=0), c)
        elif m == 2:
            r = jnp.where(sub < 4, _bcast_row(c, 1), _bcast_row(c, 5))
        else:
            r = _bcast_row(c, 3)
        zs.append(jnp.where(upper, q, k) * jnp.exp2(jnp.where(upper, c, r - c)))
        cn.append(jnp.where(upper, c + r, c))
    return zs, cn


def _big_level(m, cb, qb, kb):
    nb = m // SUBLANES
    zs, cn = [], []
    for i in range(len(cb) // (2 * nb)):
        lo = range(2 * nb * i, 2 * nb * i + nb)
        r = _bcast_row(cb[lo[-1]], SUBLANES - 1)
        for j in lo:
            zs.append(kb[j] * jnp.exp2(r - cb[j]))
            cn.append(cb[j])
        for j in range(lo[-1] + 1, lo[-1] + 1 + nb):
            zs.append(qb[j] * jnp.exp2(cb[j]))
            cn.append(cb[j] + r)
    return zs, cn


def _attn_kernel(heads, q_ref, k_ref, v_ref, lf_ref, o_ref, st_ref):
    @pl.when(pl.program_id(1) == 0)
    def _():
        st_ref[...] = jnp.zeros_like(st_ref)

    C, dk = CHUNK, q_ref.shape[2]
    group = min(LOCKSTEP, heads)
    n_groups = heads // group
    span = max(1, LOCKSTEP // group)
    n_small = int(math.log2(SUBLANES))
    sub = lax.broadcasted_iota(jnp.int32, (SUBLANES, dk), 0)
    lane = lax.broadcasted_iota(jnp.int32, (SUBLANES, C), 1)
    ti = lax.broadcasted_iota(jnp.int32, (C, C), 0)
    si = lax.broadcasted_iota(jnp.int32, (C, C), 1)
    x = ti ^ si
    small_id = jnp.full((C, C), -1, jnp.int32)
    for l in reversed(range(n_small)):
        small_id = jnp.where(x < 2 ** (l + 1), l + 1, small_id)
    small_id = jnp.where(x == 0, 0, jnp.where((ti < si) | (x >= SUBLANES), -1, small_id))
    small_id_b = _row_blocks(small_id)

    def chunk_head_group(it, carry):
        first = (it // n_groups) * span
        members = [(pl.ds(pl.multiple_of((first + cc) * C, C), C), (it % n_groups) * group + i)
                   for cc in range(span) for i in range(group)]
        n = len(members)
        qs = [q_ref[h, rows, :] for rows, h in members]
        ks = [k_ref[h, rows, :] for rows, h in members]
        qb = [_row_blocks(q) for q in qs]
        kb = [_row_blocks(k) for k in ks]
        cb = [_row_blocks(lf_ref[h, rows, :]) for rows, h in members]
        ab = [None] * n

        def scores(zs):
            z = jnp.concatenate(zs, axis=0).astype(BF16)
            return _dot_nt(z, z)

        ps = [[_dot_nt(qs[i].astype(BF16), ks[i].astype(BF16))] for i in range(n)]
        for l in range(n_small):
            for i in range(n):
                zs, cb[i] = _small_level(2 ** l, sub, cb[i], qb[i], kb[i])
                ps[i].append(scores(zs))
        for i in range(n):
            pb = [_row_blocks(p) for p in ps[i]]
            blocks = []
            for j, sid in enumerate(small_id_b):
                a = jnp.zeros_like(pb[0][j])
                for l in reversed(range(n_small + 1)):
                    a = jnp.where(sid == l, pb[l][j], a)
                blocks.append(a)
            ab[i] = blocks
        m = SUBLANES
        while m < C:
            for i in range(n):
                zs, cb[i] = _big_level(m, cb[i], qb[i], kb[i])
                pb = _row_blocks(scores(zs))
                nb = m // SUBLANES
                for blk in range(C // (2 * m)):
                    keep = (lane >= 2 * m * blk) & (lane < 2 * m * blk + m)
                    for j in range(2 * nb * blk + nb, 2 * nb * (blk + 1)):
                        ab[i][j] = jnp.where(keep, pb[j], ab[i][j])
            m *= 2
        for i, (rows, h) in enumerate(members):
            c = jnp.concatenate(cb[i], axis=0)
            a = jnp.concatenate(ab[i], axis=0).astype(BF16)
            v = v_ref[h, rows, :]
            st = st_ref[h]
            o_ref[h, rows, :] = _dot(a, v) + _dot_nt((qs[i] * jnp.exp2(c)).astype(BF16), st.astype(BF16))
            c_last = c[C - 1:C, :]
            kd = (ks[i] * jnp.exp2(c_last - c)).astype(BF16)
            st_ref[h] = st * jnp.exp2(c_last) + _dot_tn(v, kd)
        return carry

    lax.fori_loop(0, (q_ref.shape[1] // (C * span)) * n_groups, chunk_head_group, 0)


def _dense_kernel(heads, final, o_ref, g_ref, hnw_ref, wout_ref, h_ref, fnw_ref, wup_ref, cw_ref,
                  cb_ref, wdown_ref, finw_ref, out_ref, carry_ref):
    dff = wdown_ref.shape[0]

    @pl.when(pl.program_id(1) == 0)
    def _():
        carry_ref[...] = jnp.zeros_like(carry_ref)

    tm = h_ref.shape[0]
    rows_per = tm // DENSE_PARTS
    slab = 2 * LANES
    n_slabs = dff // slab
    hw = hnw_ref[...]

    def mixer_out(r0):
        rows = slice(r0, r0 + rows_per)
        parts = []
        for hh in range(heads):
            o = o_ref[hh, rows, :]
            parts.append(o * lax.rsqrt(jnp.mean(o * o, axis=-1, keepdims=True) + EPS) * hw)
        g = g_ref[rows, :].astype(F32)
        mix_in = (jnp.concatenate(parts, axis=-1) * (g * _sigmoid(g))).astype(BF16)
        h = h_ref[rows, :] + _dot(mix_in, wout_ref[...])
        return h, _rms(h, fnw_ref[...]).astype(BF16)

    up = lambda y, n: [_dot(y, wup_ref[:, base:base + slab]) for base in (n * slab, dff + n * slab)]

    sub = lax.broadcasted_iota(jnp.int32, (SUBLANES, slab), 0)

    def delayed(blocks):
        rot = [pltpu.roll(b, 1, axis=0) for b in blocks]
        return [rot[0]] + [jnp.where(sub < 1, rot[i - 1], rot[i]) for i in range(1, len(rot))]

    def conv(u, base):
        blocks = [carry_ref[:, base:base + slab]] + _row_blocks(u)
        carry_ref[:, base:base + slab] = blocks[-1]
        cw = cw_ref[:, base:base + slab]
        w0, w1, w2, b = cw[0:1], cw[1:2], cw[2:3], cb_ref[:, base:base + slab]
        inner = [w1 * x + d for x, d in zip(blocks, delayed([w0 * x for x in blocks]))]
        return jnp.concatenate([b + w2 * x + d for x, d in zip(blocks[1:], delayed(inner)[1:])], axis=0)

    order = [(p, n) for p in range(DENSE_PARTS) for n in range(n_slabs)]
    parts_hy = [mixer_out(0)]
    ups = {i: up(parts_hy[0][1], order[i][1]) for i in range(UP_AHEAD)}
    for p in range(1, DENSE_PARTS):
        parts_hy.append(mixer_out(p * rows_per))
    for i, (p, n) in enumerate(order):
        if n == 0:
            acc = parts_hy[p][0]
        u_a, u_gate = ups.pop(i)
        if i + UP_AHEAD < len(order):
            pa, na = order[i + UP_AHEAD]
            ups[i + UP_AHEAD] = up(parts_hy[pa][1], na)
        a, gate = conv(u_a, n * slab), conv(u_gate, dff + n * slab)
        s = (gate * _sigmoid(gate) * a).astype(BF16)
        acc = acc + _dot(s, wdown_ref[n * slab:(n + 1) * slab, :])
        if n + 1 == n_slabs:
            if final:
                acc = _rms(acc, finw_ref[...])
            out_ref[p * rows_per:(p + 1) * rows_per, :] = acc


def _const_spec(shape):
    nd = len(shape)
    return pl.BlockSpec(shape, lambda *_: (0,) * nd, pipeline_mode=pl.Buffered(1))


def _params(sem):
    return pltpu.CompilerParams(dimension_semantics=sem, vmem_limit_bytes=VMEM_LIMIT)


def _head_out(B, T, heads, d, tm, dtype=F32):
    return (jax.ShapeDtypeStruct((B, heads, T, d), dtype),
            pl.BlockSpec((None, heads, tm, d), lambda b, i: (b, 0, i, 0)))


def _hg_proj(h, nw, lb_logits, w_in, slot, heads):
    B, T, D = h.shape
    tm = TM_PROJ
    dim = w_in.shape[1] // 4
    dk = dim // heads
    fs, hspec = _head_out(B, T, heads, dk, tm)
    bs, _ = _head_out(B, T, heads, dk, tm, BF16)
    tok_spec = pl.BlockSpec((None, tm, dim), lambda b, i: (b, i, 0))
    return pl.pallas_call(
        functools.partial(_hg_proj_kernel, slot, heads),
        grid=(B, T // tm),
        in_specs=[pl.BlockSpec((None, tm, D), lambda b, i: (b, i, 0)), _const_spec((1, D)),
                  _const_spec(lb_logits.shape), _const_spec(w_in.shape)],
        out_specs=[hspec, hspec, hspec, hspec, tok_spec],
        out_shape=[fs, fs, bs, fs, jax.ShapeDtypeStruct((B, T, dim), BF16)],
        compiler_params=_params(("parallel", "parallel")),
        name="hg_proj",
    )(h, nw.reshape(1, D), lb_logits, w_in)


def _gla_proj(h, nw, w_main, w_lr, w_up, b_up, heads, kd, vd):
    B, T, D = h.shape
    tm = TM_PROJ
    fs, kspec = _head_out(B, T, heads, kd // heads, tm)
    vs, vspec = _head_out(B, T, heads, vd // heads, tm, BF16)
    tok_spec = pl.BlockSpec((None, tm, vd), lambda b, i: (b, i, 0))
    return pl.pallas_call(
        functools.partial(_gla_proj_kernel, heads, kd, vd),
        grid=(B, T // tm),
        in_specs=[pl.BlockSpec((None, tm, D), lambda b, i: (b, i, 0)), _const_spec((1, D)),
                  _const_spec(w_main.shape), _const_spec(w_lr.shape), _const_spec(w_up.shape),
                  _const_spec((1, kd))],
        out_specs=[kspec, kspec, vspec, kspec, tok_spec],
        out_shape=[fs, fs, vs, fs, jax.ShapeDtypeStruct((B, T, vd), BF16)],
        compiler_params=_params(("parallel", "parallel")),
        name="gla_proj",
    )(h, nw.reshape(1, D), w_main, w_lr, w_up, b_up.reshape(1, kd))


def _attention(q, k, v, lf):
    B, H, T, dk = q.shape
    dv = v.shape[-1]
    rows = CHUNK * CHUNKS_PER_STEP
    kspec = pl.BlockSpec((None, H, rows, dk), lambda b, c: (b, 0, c, 0))
    vspec = pl.BlockSpec((None, H, rows, dv), lambda b, c: (b, 0, c, 0))
    return pl.pallas_call(
        functools.partial(_attn_kernel, H),
        grid=(B, T // rows),
        in_specs=[kspec, kspec, vspec, kspec],
        out_specs=vspec,
        out_shape=jax.ShapeDtypeStruct((B, H, T, dv), F32),
        scratch_shapes=[pltpu.VMEM((H, dv, dk), F32)],
        compiler_params=_params(("parallel", "arbitrary")),
        name="chunk_attn",
    )(q, k, v, lf)


def _dense(o, g, hnw, w_out, h, fnw, w_up, conv_w, conv_b, w_down, finw, final):
    B, H, T, dv = o.shape
    D = h.shape[-1]
    dff = w_down.shape[0]
    tm = TM_DENSE
    tok = lambda d: pl.BlockSpec((None, tm, d), lambda b, i: (b, i, 0))
    return pl.pallas_call(
        functools.partial(_dense_kernel, H, final),
        grid=(B, T // tm),
        in_specs=[pl.BlockSpec((None, H, tm, dv), lambda b, i: (b, 0, i, 0)), tok(H * dv),
                  _const_spec((1, dv)), _const_spec(w_out.shape), tok(D), _const_spec((1, D)),
                  _const_spec(w_up.shape), _const_spec(conv_w.shape), _const_spec((1, 2 * dff)),
                  _const_spec(w_down.shape), _const_spec((1, D))],
        out_specs=tok(D),
        out_shape=jax.ShapeDtypeStruct((B, T, D), F32),
        scratch_shapes=[pltpu.VMEM((SUBLANES, 2 * dff), F32)],
        compiler_params=_params(("parallel", "arbitrary")),
        name="dense_tail",
    )(o, g, hnw.reshape(1, dv), w_out, h, fnw.reshape(1, D), w_up, conv_w, conv_b.reshape(1, 2 * dff),
      w_down, finw.reshape(1, D))


def kernel(x, lb_logits, hg_w_in, hg_norm_w, hg_w_out, gla_w_in, gla_w_gk_up, gla_b_gk_up, gla_norm_w,
           gla_w_out, norm_mixer_w, norm_ffn_w, ffn_w_up, ffn_conv_w, ffn_conv_b, ffn_w_down, norm_final_w):
    depth = norm_mixer_w.shape[0]
    hg_heads = hg_w_out.shape[1] // hg_norm_w.shape[-1]
    gla_vd = gla_w_out.shape[1]
    gla_heads = gla_vd // gla_norm_w.shape[-1]
    gla_kd = gla_w_gk_up.shape[-1]
    rank = gla_w_gk_up.shape[1]
    bf = lambda w: w.astype(BF16)

    h = x
    for i in range(depth):
        j = i // 2
        if i % 2 == 0:
            q, k, v, lf, g = _hg_proj(h, norm_mixer_w[i], lb_logits, bf(hg_w_in[j]), i, hg_heads)
            hnw, w_out = hg_norm_w[j], hg_w_out[j]
        else:
            n_main = 2 * gla_kd + 2 * gla_vd
            w_in = gla_w_in[j]
            w_lr = jnp.pad(w_in[:, n_main:], ((0, 0), (0, LANES - rank)))
            w_up = jnp.pad(gla_w_gk_up[j], ((0, LANES - rank), (0, 0)))
            q, k, v, lf, g = _gla_proj(h, norm_mixer_w[i], bf(w_in[:, :n_main]), bf(w_lr), bf(w_up),
                                       gla_b_gk_up[j], gla_heads, gla_kd, gla_vd)
            hnw, w_out = gla_norm_w[j], gla_w_out[j]
        o = _attention(q, k, v, lf)
        h = _dense(o, g, hnw, bf(w_out), h, norm_ffn_w[i], bf(ffn_w_up[i]), ffn_conv_w[i], ffn_conv_b[i],
                   bf(ffn_w_down[i]), norm_final_w, i == depth - 1)
    return h
```

```python
import functools
import math

import jax
import jax.numpy as jnp
from jax import lax
from jax.experimental import pallas as pl
from jax.experimental.pallas import tpu as pltpu

EPS = 1e-6
GLA_GATE_NORMALIZER = 16.0
LANES = 128
SUBLANES = 8
VMEM_LIMIT = 56 * 1024 * 1024
LOG2E = math.log2(math.e)
CHUNK = 128
CHUNKS_PER_STEP = 8
LOCKSTEP = 8
TM_PROJ = 1024
TM_DENSE = 512
DENSE_PARTS = 2
UP_AHEAD = 1

F32 = jnp.float32
BF16 = jnp.bfloat16


def _rms(x, w):
    return x * lax.rsqrt(jnp.mean(x * x, axis=-1, keepdims=True) + EPS) * w


def _sigmoid(x):
    return 1.0 / (1.0 + jnp.exp(-x))


def _dot(a, b):
    return jnp.dot(a, b, preferred_element_type=F32)


def _dot_nt(a, b):
    return lax.dot_general(a, b, (((1,), (1,)), ((), ())), preferred_element_type=F32)


def _dot_tn(a, b):
    return lax.dot_general(a, b, (((0,), (0,)), ((), ())), preferred_element_type=F32)


def _store_heads(ref, val, heads):
    d = val.shape[-1] // heads
    for h in range(heads):
        ref[h] = val[:, h * d:(h + 1) * d].astype(ref.dtype)


def _row_blocks(x):
    return [x[SUBLANES * j:SUBLANES * (j + 1)] for j in range(x.shape[0] // SUBLANES)]


def _bcast_row(blk, r):
    return jnp.broadcast_to(blk[r:r + 1], blk.shape)


def _hg_proj_kernel(slot, heads, x_ref, nw_ref, lbl_ref, w_ref, q_ref, k_ref, v_ref, lf_ref, g_ref):
    y = _rms(x_ref[...], nw_ref[...]).astype(BF16)
    dim = g_ref.shape[-1]
    dk = dim // heads
    sect = lambda s: _dot(y, w_ref[:, s * dim:(s + 1) * dim])
    l = lbl_ref[...]
    e = jnp.exp(l - jnp.max(l, axis=0, keepdims=True))
    lb = jnp.sum(e[:slot + 1], axis=0, keepdims=True) / jnp.sum(e, axis=0, keepdims=True)
    q = sect(0)
    _store_heads(q_ref, q * _sigmoid(q) * (dk ** -0.5), heads)
    forget = lb + (1.0 - lb) * _sigmoid(sect(1))
    _store_heads(k_ref, 1.0 - forget, heads)
    _store_heads(lf_ref, jnp.log(forget) * LOG2E, heads)
    _store_heads(v_ref, sect(2), heads)
    g_ref[...] = sect(3)


def _gla_proj_kernel(heads, kd, vd, x_ref, nw_ref, w_ref, wlr_ref, wup_ref, bup_ref,
                     q_ref, k_ref, v_ref, lf_ref, g_ref):
    y = _rms(x_ref[...], nw_ref[...]).astype(BF16)
    dk = kd // heads
    _store_heads(q_ref, _dot(y, w_ref[:, 0:kd]) * (dk ** -0.5), heads)
    _store_heads(k_ref, _dot(y, w_ref[:, kd:2 * kd]), heads)
    _store_heads(v_ref, _dot(y, w_ref[:, 2 * kd:2 * kd + vd]), heads)
    g_ref[...] = _dot(y, w_ref[:, 2 * kd + vd:2 * kd + 2 * vd])
    lr = _dot(y, wlr_ref[...]).astype(BF16)
    gk = _dot(lr, wup_ref[...]) + bup_ref[...]
    log_sig = jnp.minimum(gk, 0.0) - jnp.log(1.0 + jnp.exp(-jnp.abs(gk)))
    _store_heads(lf_ref, log_sig * (LOG2E / GLA_GATE_NORMALIZER), heads)


def _small_level(m, sub, cb, qb, kb):
    upper = (sub & m) != 0
    zs, cn = [], []
    for c, q, k in zip(cb, qb, kb):
        if m == 1:
            r = jnp.where(upper, pltpu.roll(c, 1, axis=0), c)
        elif m == 2:
            r = jnp.where(sub < 4, _bcast_row(c, 1), _bcast_row(c, 5))
        else:
            r = _bcast_row(c, 3)
        zs.append(jnp.where(upper, q, k) * jnp.exp2(jnp.where(upper, c, r - c)))
        cn.append(jnp.where(upper, c + r, c))
    return zs, cn


def _big_level(m, cb, qb, kb):
    nb = m // SUBLANES
    zs, cn = [], []
    for i in range(len(cb) // (2 * nb)):
        lo = range(2 * nb * i, 2 * nb * i + nb)
        r = _bcast_row(cb[lo[-1]], SUBLANES - 1)
        for j in lo:
            zs.append(kb[j] * jnp.exp2(r - cb[j]))
            cn.append(cb[j])
        for j in range(lo[-1] + 1, lo[-1] + 1 + nb):
            zs.append(qb[j] * jnp.exp2(cb[j]))
            cn.append(cb[j] + r)
    return zs, cn


def _attn_kernel(heads, q_ref, k_ref, v_ref, lf_ref, o_ref, st_ref):
    @pl.when(pl.program_id(1) == 0)
    def _():
        st_ref[...] = jnp.zeros_like(st_ref)

    C, dk = CHUNK, q_ref.shape[2]
    group = min(LOCKSTEP, heads)
    n_groups = heads // group
    span = max(1, LOCKSTEP // group)
    n_small = int(math.log2(SUBLANES))
    sub = lax.broadcasted_iota(jnp.int32, (SUBLANES, dk), 0)
    lane = lax.broadcasted_iota(jnp.int32, (SUBLANES, C), 1)
    ti = lax.broadcasted_iota(jnp.int32, (C, C), 0)
    si = lax.broadcasted_iota(jnp.int32, (C, C), 1)
    x = ti ^ si
    small_id = jnp.full((C, C), -1, jnp.int32)
    for l in reversed(range(n_small)):
        small_id = jnp.where(x < 2 ** (l + 1), l + 1, small_id)
    small_id = jnp.where(x == 0, 0, jnp.where((ti < si) | (x >= SUBLANES), -1, small_id))
    small_id_b = _row_blocks(small_id)

    def chunk_head_group(it, carry):
        first = (it // n_groups) * span
        members = [(pl.ds(pl.multiple_of((first + cc) * C, C), C), (it % n_groups) * group + i)
                   for cc in range(span) for i in range(group)]
        n = len(members)
        qs = [q_ref[h, rows, :] for rows, h in members]
        ks = [k_ref[h, rows, :] for rows, h in members]
        qb = [_row_blocks(q) for q in qs]
        kb = [_row_blocks(k) for k in ks]
        cb = [_row_blocks(lf_ref[h, rows, :]) for rows, h in members]
        ab = [None] * n

        def scores(zs):
            z = jnp.concatenate(zs, axis=0).astype(BF16)
            return _dot_nt(z, z)

        ps = [[_dot_nt(qs[i].astype(BF16), ks[i].astype(BF16))] for i in range(n)]
        for l in range(n_small):
            for i in range(n):
                zs, cb[i] = _small_level(2 ** l, sub, cb[i], qb[i], kb[i])
                ps[i].append(scores(zs))
        for i in range(n):
            pb = [_row_blocks(p) for p in ps[i]]
            blocks = []
            for j, sid in enumerate(small_id_b):
                a = jnp.zeros_like(pb[0][j])
                for l in reversed(range(n_small + 1)):
                    a = jnp.where(sid == l, pb[l][j], a)
                blocks.append(a)
            ab[i] = blocks
        m = SUBLANES
        while m < C:
            for i in range(n):
                zs, cb[i] = _big_level(m, cb[i], qb[i], kb[i])
                pb = _row_blocks(scores(zs))
                nb = m // SUBLANES
                for blk in range(C // (2 * m)):
                    keep = (lane >= 2 * m * blk) & (lane < 2 * m * blk + m)
                    for j in range(2 * nb * blk + nb, 2 * nb * (blk + 1)):
                        ab[i][j] = jnp.where(keep, pb[j], ab[i][j])
            m *= 2
        for i, (rows, h) in enumerate(members):
            c = jnp.concatenate(cb[i], axis=0)
            a = jnp.concatenate(ab[i], axis=0).astype(BF16)
            v = v_ref[h, rows, :]
            st = st_ref[h]
            o_ref[h, rows, :] = _dot(a, v) + _dot_nt((qs[i] * jnp.exp2(c)).astype(BF16), st.astype(BF16))
            c_last = c[C - 1:C, :]
            kd = (ks[i] * jnp.exp2(c_last - c)).astype(BF16)
            st_ref[h] = st * jnp.exp2(c_last) + _dot_tn(v, kd)
        return carry

    lax.fori_loop(0, (q_ref.shape[1] // (C * span)) * n_groups, chunk_head_group, 0)


def _dense_kernel(heads, final, o_ref, g_ref, hnw_ref, wout_ref, h_ref, fnw_ref, wup_ref, cw_ref,
                  cb_ref, wdown_ref, finw_ref, out_ref, carry_ref):
    dff = wdown_ref.shape[0]

    @pl.when(pl.program_id(1) == 0)
    def _():
        carry_ref[...] = jnp.zeros_like(carry_ref)

    tm = h_ref.shape[0]
    rows_per = tm // DENSE_PARTS
    slab = 2 * LANES
    n_slabs = dff // slab
    hw = hnw_ref[...]

    def mixer_out(r0):
        rows = slice(r0, r0 + rows_per)
        parts = []
        for hh in range(heads):
            o = o_ref[hh, rows, :]
            parts.append(o * lax.rsqrt(jnp.mean(o * o, axis=-1, keepdims=True) + EPS) * hw)
        g = g_ref[rows, :]
        mix_in = (jnp.concatenate(parts, axis=-1) * (g * _sigmoid(g))).astype(BF16)
        h = h_ref[rows, :] + _dot(mix_in, wout_ref[...])
        return h, _rms(h, fnw_ref[...]).astype(BF16)

    up = lambda y, n: [_dot(y, wup_ref[:, base:base + slab]) for base in (n * slab, dff + n * slab)]

    sub = lax.broadcasted_iota(jnp.int32, (SUBLANES, slab), 0)

    def delayed(blocks):
        rot = [pltpu.roll(b, 1, axis=0) for b in blocks]
        return [rot[0]] + [jnp.where(sub < 1, rot[i - 1], rot[i]) for i in range(1, len(rot))]

    def conv(u, base):
        blocks = [carry_ref[:, base:base + slab]] + _row_blocks(u)
        carry_ref[:, base:base + slab] = blocks[-1]
        cw = cw_ref[:, base:base + slab]
        w0, w1, w2, b = cw[0:1], cw[1:2], cw[2:3], cb_ref[:, base:base + slab]
        inner = [w1 * x + d for x, d in zip(blocks, delayed([w0 * x for x in blocks]))]
        return jnp.concatenate([b + w2 * x + d for x, d in zip(blocks[1:], delayed(inner)[1:])], axis=0)

    order = [(p, n) for p in range(DENSE_PARTS) for n in range(n_slabs)]
    parts_hy = [mixer_out(0)]
    ups = {i: up(parts_hy[0][1], order[i][1]) for i in range(UP_AHEAD)}
    for p in range(1, DENSE_PARTS):
        parts_hy.append(mixer_out(p * rows_per))
    for i, (p, n) in enumerate(order):
        if n == 0:
            acc = parts_hy[p][0]
        u_a, u_gate = ups.pop(i)
        if i + UP_AHEAD < len(order):
            pa, na = order[i + UP_AHEAD]
            ups[i + UP_AHEAD] = up(parts_hy[pa][1], na)
        a, gate = conv(u_a, n * slab), conv(u_gate, dff + n * slab)
        s = (gate * _sigmoid(gate) * a).astype(BF16)
        acc = acc + _dot(s, wdown_ref[n * slab:(n + 1) * slab, :])
        if n + 1 == n_slabs:
            if final:
                acc = _rms(acc, finw_ref[...])
            out_ref[p * rows_per:(p + 1) * rows_per, :] = acc


def _const_spec(shape):
    nd = len(shape)
    return pl.BlockSpec(shape, lambda *_: (0,) * nd, pipeline_mode=pl.Buffered(1))


def _params(sem):
    return pltpu.CompilerParams(dimension_semantics=sem, vmem_limit_bytes=VMEM_LIMIT)


def _head_out(B, T, heads, d, tm, dtype=F32):
    return (jax.ShapeDtypeStruct((B, heads, T, d), dtype),
            pl.BlockSpec((None, heads, tm, d), lambda b, i: (b, 0, i, 0)))


def _hg_proj(h, nw, lb_logits, w_in, slot, heads):
    B, T, D = h.shape
    tm = TM_PROJ
    dim = w_in.shape[1] // 4
    dk = dim // heads
    fs, hspec = _head_out(B, T, heads, dk, tm)
    bs, _ = _head_out(B, T, heads, dk, tm, BF16)
    tok_spec = pl.BlockSpec((None, tm, dim), lambda b, i: (b, i, 0))
    return pl.pallas_call(
        functools.partial(_hg_proj_kernel, slot, heads),
        grid=(B, T // tm),
        in_specs=[pl.BlockSpec((None, tm, D), lambda b, i: (b, i, 0)), _const_spec((1, D)),
                  _const_spec(lb_logits.shape), _const_spec(w_in.shape)],
        out_specs=[hspec, hspec, hspec, hspec, tok_spec],
        out_shape=[fs, fs, bs, fs, jax.ShapeDtypeStruct((B, T, dim), F32)],
        compiler_params=_params(("parallel", "parallel")),
        name="hg_proj",
    )(h, nw.reshape(1, D), lb_logits, w_in)


def _gla_proj(h, nw, w_main, w_lr, w_up, b_up, heads, kd, vd):
    B, T, D = h.shape
    tm = TM_PROJ
    fs, kspec = _head_out(B, T, heads, kd // heads, tm)
    vs, vspec = _head_out(B, T, heads, vd // heads, tm, BF16)
    tok_spec = pl.BlockSpec((None, tm, vd), lambda b, i: (b, i, 0))
    return pl.pallas_call(
        functools.partial(_gla_proj_kernel, heads, kd, vd),
        grid=(B, T // tm),
        in_specs=[pl.BlockSpec((None, tm, D), lambda b, i: (b, i, 0)), _const_spec((1, D)),
                  _const_spec(w_main.shape), _const_spec(w_lr.shape), _const_spec(w_up.shape),
                  _const_spec((1, kd))],
        out_specs=[kspec, kspec, vspec, kspec, tok_spec],
        out_shape=[fs, fs, vs, fs, jax.ShapeDtypeStruct((B, T, vd), F32)],
        compiler_params=_params(("parallel", "parallel")),
        name="gla_proj",
    )(h, nw.reshape(1, D), w_main, w_lr, w_up, b_up.reshape(1, kd))


def _attention(q, k, v, lf):
    B, H, T, dk = q.shape
    dv = v.shape[-1]
    rows = CHUNK * CHUNKS_PER_STEP
    kspec = pl.BlockSpec((None, H, rows, dk), lambda b, c: (b, 0, c, 0))
    vspec = pl.BlockSpec((None, H, rows, dv), lambda b, c: (b, 0, c, 0))
    return pl.pallas_call(
        functools.partial(_attn_kernel, H),
        grid=(B, T // rows),
        in_specs=[kspec, kspec, vspec, kspec],
        out_specs=vspec,
        out_shape=jax.ShapeDtypeStruct((B, H, T, dv), F32),
        scratch_shapes=[pltpu.VMEM((H, dv, dk), F32)],
        compiler_params=_params(("parallel", "arbitrary")),
        name="chunk_attn",
    )(q, k, v, lf)


def _dense(o, g, hnw, w_out, h, fnw, w_up, conv_w, conv_b, w_down, finw, final):
    B, H, T, dv = o.shape
    D = h.shape[-1]
    dff = w_down.shape[0]
    tm = TM_DENSE
    tok = lambda d: pl.BlockSpec((None, tm, d), lambda b, i: (b, i, 0))
    return pl.pallas_call(
        functools.partial(_dense_kernel, H, final),
        grid=(B, T // tm),
        in_specs=[pl.BlockSpec((None, H, tm, dv), lambda b, i: (b, 0, i, 0)), tok(H * dv),
                  _const_spec((1, dv)), _const_spec(w_out.shape), tok(D), _const_spec((1, D)),
                  _const_spec(w_up.shape), _const_spec(conv_w.shape), _const_spec((1, 2 * dff)),
                  _const_spec(w_down.shape), _const_spec((1, D))],
        out_specs=tok(D),
        out_shape=jax.ShapeDtypeStruct((B, T, D), F32),
        scratch_shapes=[pltpu.VMEM((SUBLANES, 2 * dff), F32)],
        compiler_params=_params(("parallel", "arbitrary")),
        name="dense_tail",
    )(o, g, hnw.reshape(1, dv), w_out, h, fnw.reshape(1, D), w_up, conv_w, conv_b.reshape(1, 2 * dff),
      w_down, finw.reshape(1, D))


def kernel(x, lb_logits, hg_w_in, hg_norm_w, hg_w_out, gla_w_in, gla_w_gk_up, gla_b_gk_up, gla_norm_w,
           gla_w_out, norm_mixer_w, norm_ffn_w, ffn_w_up, ffn_conv_w, ffn_conv_b, ffn_w_down, norm_final_w):
    depth = norm_mixer_w.shape[0]
    hg_heads = hg_w_out.shape[1] // hg_norm_w.shape[-1]
    gla_vd = gla_w_out.shape[1]
    gla_heads = gla_vd // gla_norm_w.shape[-1]
    gla_kd = gla_w_gk_up.shape[-1]
    rank = gla_w_gk_up.shape[1]
    bf = lambda w: w.astype(BF16)

    h = x
    for i in range(depth):
        j = i // 2
        if i % 2 == 0:
            q, k, v, lf, g = _hg_proj(h, norm_mixer_w[i], lb_logits, bf(hg_w_in[j]), i, hg_heads)
            hnw, w_out = hg_norm_w[j], hg_w_out[j]
        else:
            n_main = 2 * gla_kd + 2 * gla_vd
            w_in = gla_w_in[j]
            w_lr = jnp.pad(w_in[:, n_main:], ((0, 0), (0, LANES - rank)))
            w_up = jnp.pad(gla_w_gk_up[j], ((0, LANES - rank), (0, 0)))
            q, k, v, lf, g = _gla_proj(h, norm_mixer_w[i], bf(w_in[:, :n_main]), bf(w_lr), bf(w_up),
                                       gla_b_gk_up[j], gla_heads, gla_kd, gla_vd)
            hnw, w_out = gla_norm_w[j], gla_w_out[j]
        o = _attention(q, k, v, lf)
        h = _dense(o, g, hnw, bf(w_out), h, norm_ffn_w[i], bf(ffn_w_up[i]), ffn_conv_w[i], ffn_conv_b[i],
                   bf(ffn_w_down[i]), norm_final_w, i == depth - 1)
    return h
```

```python
import functools
import math

import jax
import jax.numpy as jnp
from jax import lax
from jax.experimental import pallas as pl
from jax.experimental.pallas import tpu as pltpu

EPS = 1e-6
GLA_GATE_NORMALIZER = 16.0
LANES = 128
SUBLANES = 8
VMEM_LIMIT = 56 * 1024 * 1024
LOG2E = math.log2(math.e)
CHUNK = 128
CHUNKS_PER_STEP = 8
LOCKSTEP = 8
TM_PROJ = 1024
TM_DENSE = 512
DENSE_PARTS = 2
UP_AHEAD = 1

F32 = jnp.float32
BF16 = jnp.bfloat16


def _rms(x, w):
    return x * lax.rsqrt(jnp.mean(x * x, axis=-1, keepdims=True) + EPS) * w


def _sigmoid(x):
    return 1.0 / (1.0 + jnp.exp(-x))


def _dot(a, b):
    return jnp.dot(a, b, preferred_element_type=F32)


def _dot_nt(a, b):
    return lax.dot_general(a, b, (((1,), (1,)), ((), ())), preferred_element_type=F32)


def _dot_tn(a, b):
    return lax.dot_general(a, b, (((0,), (0,)), ((), ())), preferred_element_type=F32)


def _store_heads(ref, val, heads):
    d = val.shape[-1] // heads
    for h in range(heads):
        ref[h] = val[:, h * d:(h + 1) * d].astype(ref.dtype)


def _row_blocks(x):
    return [x[SUBLANES * j:SUBLANES * (j + 1)] for j in range(x.shape[0] // SUBLANES)]


def _bcast_row(blk, r):
    return jnp.broadcast_to(blk[r:r + 1], blk.shape)


def _hg_proj_kernel(slot, heads, x_ref, nw_ref, lbl_ref, w_ref, q_ref, k_ref, v_ref, lf_ref, g_ref):
    y = _rms(x_ref[...], nw_ref[...]).astype(BF16)
    dim = g_ref.shape[-1]
    dk = dim // heads
    sect = lambda s: _dot(y, w_ref[:, s * dim:(s + 1) * dim])
    l = lbl_ref[...]
    e = jnp.exp(l - jnp.max(l, axis=0, keepdims=True))
    lb = jnp.sum(e[:slot + 1], axis=0, keepdims=True) / jnp.sum(e, axis=0, keepdims=True)
    q = sect(0)
    _store_heads(q_ref, q * _sigmoid(q) * (dk ** -0.5), heads)
    forget = lb + (1.0 - lb) * _sigmoid(sect(1))
    _store_heads(k_ref, 1.0 - forget, heads)
    _store_heads(lf_ref, jnp.log(forget) * LOG2E, heads)
    _store_heads(v_ref, sect(2), heads)
    g_ref[...] = sect(3)


def _gla_proj_kernel(heads, kd, vd, x_ref, nw_ref, w_ref, wlr_ref, wup_ref, bup_ref,
                     q_ref, k_ref, v_ref, lf_ref, g_ref):
    y = _rms(x_ref[...], nw_ref[...]).astype(BF16)
    dk = kd // heads
    _store_heads(q_ref, _dot(y, w_ref[:, 0:kd]) * (dk ** -0.5), heads)
    _store_heads(k_ref, _dot(y, w_ref[:, kd:2 * kd]), heads)
    _store_heads(v_ref, _dot(y, w_ref[:, 2 * kd:2 * kd + vd]), heads)
    g_ref[...] = _dot(y, w_ref[:, 2 * kd + vd:2 * kd + 2 * vd])
    lr = _dot(y, wlr_ref[...]).astype(BF16)
    gk = _dot(lr, wup_ref[...]) + bup_ref[...]
    log_sig = jnp.minimum(gk, 0.0) - jnp.log(1.0 + jnp.exp(-jnp.abs(gk)))
    _store_heads(lf_ref, log_sig * (LOG2E / GLA_GATE_NORMALIZER), heads)


def _small_level(m, sub, cb, qb, kb):
    upper = (sub & m) != 0
    zs, cn = [], []
    for c, q, k in zip(cb, qb, kb):
        if m == 1:
            r = jnp.where(upper, pltpu.roll(c, 1, axis=0), c)
        elif m == 2:
            r = jnp.where(sub < 4, _bcast_row(c, 1), _bcast_row(c, 5))
        else:
            r = _bcast_row(c, 3)
        zs.append(jnp.where(upper, q, k) * jnp.exp2(jnp.where(upper, c, r - c)))
        cn.append(jnp.where(upper, c + r, c))
    return zs, cn


def _big_level(m, cb, qb, kb):
    nb = m // SUBLANES
    zs, cn = [], []
    for i in range(len(cb) // (2 * nb)):
        lo = range(2 * nb * i, 2 * nb * i + nb)
        r = _bcast_row(cb[lo[-1]], SUBLANES - 1)
        for j in lo:
            zs.append(kb[j] * jnp.exp2(r - cb[j]))
            cn.append(cb[j])
        for j in range(lo[-1] + 1, lo[-1] + 1 + nb):
            zs.append(qb[j] * jnp.exp2(cb[j]))
            cn.append(cb[j] + r)
    return zs, cn


def _attn_kernel(heads, q_ref, k_ref, v_ref, lf_ref, o_ref, st_ref):
    @pl.when(pl.program_id(1) == 0)
    def _():
        st_ref[...] = jnp.zeros_like(st_ref)

    C, dk = CHUNK, q_ref.shape[2]
    group = min(LOCKSTEP, heads)
    n_groups = heads // group
    span = max(1, LOCKSTEP // group)
    n_small = int(math.log2(SUBLANES))
    sub = lax.broadcasted_iota(jnp.int32, (SUBLANES, dk), 0)
    lane = lax.broadcasted_iota(jnp.int32, (SUBLANES, C), 1)
    ti = lax.broadcasted_iota(jnp.int32, (C, C), 0)
    si = lax.broadcasted_iota(jnp.int32, (C, C), 1)
    x = ti ^ si
    small_id = jnp.full((C, C), -1, jnp.int32)
    for l in reversed(range(n_small)):
        small_id = jnp.where(x < 2 ** (l + 1), l + 1, small_id)
    small_id = jnp.where(x == 0, 0, jnp.where((ti < si) | (x >= SUBLANES), -1, small_id))
    small_id_b = _row_blocks(small_id)

    def chunk_head_group(it, carry):
        first = (it // n_groups) * span
        members = [(pl.ds(pl.multiple_of((first + cc) * C, C), C), (it % n_groups) * group + i)
                   for cc in range(span) for i in range(group)]
        n = len(members)
        qs = [q_ref[h, rows, :] for rows, h in members]
        ks = [k_ref[h, rows, :] for rows, h in members]
        qb = [_row_blocks(q) for q in qs]
        kb = [_row_blocks(k) for k in ks]
        cb = [_row_blocks(lf_ref[h, rows, :]) for rows, h in members]
        ab = [None] * n

        def scores(zs):
            z = jnp.concatenate(zs, axis=0).astype(BF16)
            return _dot_nt(z, z)

        ps = [[_dot_nt(qs[i].astype(BF16), ks[i].astype(BF16))] for i in range(n)]
        for l in range(n_small):
            for i in range(n):
                zs, cb[i] = _small_level(2 ** l, sub, cb[i], qb[i], kb[i])
                ps[i].append(scores(zs))
        for i in range(n):
            pb = [_row_blocks(p) for p in ps[i]]
            blocks = []
            for j, sid in enumerate(small_id_b):
                a = jnp.zeros_like(pb[0][j])
                for l in reversed(range(n_small + 1)):
                    a = jnp.where(sid == l, pb[l][j], a)
                blocks.append(a)
            ab[i] = blocks
        m = SUBLANES
        while m < C:
            for i in range(n):
                zs, cb[i] = _big_level(m, cb[i], qb[i], kb[i])
                pb = _row_blocks(scores(zs))
                nb = m // SUBLANES
                for blk in range(C // (2 * m)):
                    keep = (lane >= 2 * m * blk) & (lane < 2 * m * blk + m)
                    for j in range(2 * nb * blk + nb, 2 * nb * (blk + 1)):
                        ab[i][j] = jnp.where(keep, pb[j], ab[i][j])
            m *= 2
        for i, (rows, h) in enumerate(members):
            c = jnp.concatenate(cb[i], axis=0)
            a = jnp.concatenate(ab[i], axis=0).astype(BF16)
            v = v_ref[h, rows, :]
            st = st_ref[h]
            o_ref[h, rows, :] = _dot(a, v) + _dot_nt((qs[i] * jnp.exp2(c)).astype(BF16), st.astype(BF16))
            c_last = c[C - 1:C, :]
            kd = (ks[i] * jnp.exp2(c_last - c)).astype(BF16)
            st_ref[h] = st * jnp.exp2(c_last) + _dot_tn(v, kd)
        return carry

    lax.fori_loop(0, (q_ref.shape[1] // (C * span)) * n_groups, chunk_head_group, 0)


def _dense_kernel(heads, final, o_ref, g_ref, hnw_ref, wout_ref, h_ref, fnw_ref, wup_ref, cw_ref,
                  cb_ref, wdown_ref, finw_ref, out_ref, carry_ref):
    dff = wdown_ref.shape[0]

    @pl.when(pl.program_id(1) == 0)
    def _():
        carry_ref[...] = jnp.zeros_like(carry_ref)

    tm = h_ref.shape[0]
    rows_per = tm // DENSE_PARTS
    slab = 2 * LANES
    n_slabs = dff // slab
    hw = hnw_ref[...]

    def mixer_out(r0):
        rows = slice(r0, r0 + rows_per)
        parts = []
        for hh in range(heads):
            o = o_ref[hh, rows, :]
            parts.append(o * lax.rsqrt(jnp.mean(o * o, axis=-1, keepdims=True) + EPS) * hw)
        g = g_ref[rows, :]
        mix_in = (jnp.concatenate(parts, axis=-1) * (g * _sigmoid(g))).astype(BF16)
        h = h_ref[rows, :] + _dot(mix_in, wout_ref[...])
        return h, _rms(h, fnw_ref[...]).astype(BF16)

    up = lambda y, n: [_dot(y, wup_ref[:, base:base + slab]) for base in (n * slab, dff + n * slab)]

    sub = lax.broadcasted_iota(jnp.int32, (SUBLANES, slab), 0)

    def delayed(blocks):
        rot = [pltpu.roll(b, 1, axis=0) for b in blocks]
        return [rot[0]] + [jnp.where(sub < 1, rot[i - 1], rot[i]) for i in range(1, len(rot))]

    def conv(u, base):
        blocks = [carry_ref[:, base:base + slab]] + _row_blocks(u)
        carry_ref[:, base:base + slab] = blocks[-1]
        cw = cw_ref[:, base:base + slab]
        w0, w1, w2, b = cw[0:1], cw[1:2], cw[2:3], cb_ref[:, base:base + slab]
        inner = [w1 * x + d for x, d in zip(blocks, delayed([w0 * x for x in blocks]))]
        return jnp.concatenate([b + w2 * x + d for x, d in zip(blocks[1:], delayed(inner)[1:])], axis=0)

    order = [(p, n) for p in range(DENSE_PARTS) for n in range(n_slabs)]
    parts_hy = [mixer_out(0)]
    ups = {i: up(parts_hy[0][1], order[i][1]) for i in range(UP_AHEAD)}
    for p in range(1, DENSE_PARTS):
        parts_hy.append(mixer_out(p * rows_per))
    for i, (p, n) in enumerate(order):
        if n == 0:
            acc = parts_hy[p][0]
        u_a, u_gate = ups.pop(i)
        if i + UP_AHEAD < len(order):
            pa, na = order[i + UP_AHEAD]
            ups[i + UP_AHEAD] = up(parts_hy[pa][1], na)
        a, gate = conv(u_a, n * slab), conv(u_gate, dff + n * slab)
        s = (gate * _sigmoid(gate) * a).astype(BF16)
        acc = acc + _dot(s, wdown_ref[n * slab:(n + 1) * slab, :])
        if n + 1 == n_slabs:
            if final:
                acc = _rms(acc, finw_ref[...])
            out_ref[p * rows_per:(p + 1) * rows_per, :] = acc


def _const_spec(shape):
    nd = len(shape)
    return pl.BlockSpec(shape, lambda *_: (0,) * nd, pipeline_mode=pl.Buffered(1))


def _params(sem):
    return pltpu.CompilerParams(dimension_semantics=sem, vmem_limit_bytes=VMEM_LIMIT)


def _head_out(B, T, heads, d, tm, dtype=F32):
    return (jax.ShapeDtypeStruct((B, heads, T, d), dtype),
            pl.BlockSpec((None, heads, tm, d), lambda b, i: (b, 0, i, 0)))


def _hg_proj(h, nw, lb_logits, w_in, slot, heads):
    B, T, D = h.shape
    tm = TM_PROJ
    dim = w_in.shape[1] // 4
    dk = dim // heads
    assert T % tm == 0 and dk % LANES == 0
    fs, hspec = _head_out(B, T, heads, dk, tm)
    bs, _ = _head_out(B, T, heads, dk, tm, BF16)
    tok_spec = pl.BlockSpec((None, tm, dim), lambda b, i: (b, i, 0))
    return pl.pallas_call(
        functools.partial(_hg_proj_kernel, slot, heads),
        grid=(B, T // tm),
        in_specs=[pl.BlockSpec((None, tm, D), lambda b, i: (b, i, 0)), _const_spec((1, D)),
                  _const_spec(lb_logits.shape), _const_spec(w_in.shape)],
        out_specs=[hspec, hspec, hspec, hspec, tok_spec],
        out_shape=[fs, fs, bs, fs, jax.ShapeDtypeStruct((B, T, dim), F32)],
        compiler_params=_params(("parallel", "parallel")),
        name="hg_proj",
    )(h, nw.reshape(1, D), lb_logits, w_in)


def _gla_proj(h, nw, w_main, w_lr, w_up, b_up, heads, kd, vd):
    B, T, D = h.shape
    tm = TM_PROJ
    assert T % tm == 0 and (kd // heads) % LANES == 0 and (vd // heads) % LANES == 0
    fs, kspec = _head_out(B, T, heads, kd // heads, tm)
    vs, vspec = _head_out(B, T, heads, vd // heads, tm, BF16)
    tok_spec = pl.BlockSpec((None, tm, vd), lambda b, i: (b, i, 0))
    return pl.pallas_call(
        functools.partial(_gla_proj_kernel, heads, kd, vd),
        grid=(B, T // tm),
        in_specs=[pl.BlockSpec((None, tm, D), lambda b, i: (b, i, 0)), _const_spec((1, D)),
                  _const_spec(w_main.shape), _const_spec(w_lr.shape), _const_spec(w_up.shape),
                  _const_spec((1, kd))],
        out_specs=[kspec, kspec, vspec, kspec, tok_spec],
        out_shape=[fs, fs, vs, fs, jax.ShapeDtypeStruct((B, T, vd), F32)],
        compiler_params=_params(("parallel", "parallel")),
        name="gla_proj",
    )(h, nw.reshape(1, D), w_main, w_lr, w_up, b_up.reshape(1, kd))


def _attention(q, k, v, lf):
    B, H, T, dk = q.shape
    dv = v.shape[-1]
    rows = CHUNK * CHUNKS_PER_STEP
    assert T % rows == 0 and dk % LANES == 0 and dv % LANES == 0
    assert (H % LOCKSTEP == 0) or (LOCKSTEP % H == 0 and CHUNKS_PER_STEP % (LOCKSTEP // H) == 0)
    kspec = pl.BlockSpec((None, H, rows, dk), lambda b, c: (b, 0, c, 0))
    vspec = pl.BlockSpec((None, H, rows, dv), lambda b, c: (b, 0, c, 0))
    return pl.pallas_call(
        functools.partial(_attn_kernel, H),
        grid=(B, T // rows),
        in_specs=[kspec, kspec, vspec, kspec],
        out_specs=vspec,
        out_shape=jax.ShapeDtypeStruct((B, H, T, dv), F32),
        scratch_shapes=[pltpu.VMEM((H, dv, dk), F32)],
        compiler_params=_params(("parallel", "arbitrary")),
        name="chunk_attn",
    )(q, k, v, lf)


def _dense(o, g, hnw, w_out, h, fnw, w_up, conv_w, conv_b, w_down, finw, final):
    B, H, T, dv = o.shape
    D = h.shape[-1]
    dff = w_down.shape[0]
    tm = TM_DENSE
    assert T % tm == 0 and (tm // DENSE_PARTS) % SUBLANES == 0 and dff % (2 * LANES) == 0 and dv % LANES == 0
    tok = lambda d: pl.BlockSpec((None, tm, d), lambda b, i: (b, i, 0))
    return pl.pallas_call(
        functools.partial(_dense_kernel, H, final),
        grid=(B, T // tm),
        in_specs=[pl.BlockSpec((None, H, tm, dv), lambda b, i: (b, 0, i, 0)), tok(H * dv),
                  _const_spec((1, dv)), _const_spec(w_out.shape), tok(D), _const_spec((1, D)),
                  _const_spec(w_up.shape), _const_spec(conv_w.shape), _const_spec((1, 2 * dff)),
                  _const_spec(w_down.shape), _const_spec((1, D))],
        out_specs=tok(D),
        out_shape=jax.ShapeDtypeStruct((B, T, D), F32),
        scratch_shapes=[pltpu.VMEM((SUBLANES, 2 * dff), F32)],
        compiler_params=_params(("parallel", "arbitrary")),
        name="dense_tail",
    )(o, g, hnw.reshape(1, dv), w_out, h, fnw.reshape(1, D), w_up, conv_w, conv_b.reshape(1, 2 * dff),
      w_down, finw.reshape(1, D))


def kernel(x, lb_logits, hg_w_in, hg_norm_w, hg_w_out, gla_w_in, gla_w_gk_up, gla_b_gk_up, gla_norm_w,
           gla_w_out, norm_mixer_w, norm_ffn_w, ffn_w_up, ffn_conv_w, ffn_conv_b, ffn_w_down, norm_final_w):
    depth = norm_mixer_w.shape[0]
    hg_heads = hg_w_out.shape[1] // hg_norm_w.shape[-1]
    gla_vd = gla_w_out.shape[1]
    gla_heads = gla_vd // gla_norm_w.shape[-1]
    gla_kd = gla_w_gk_up.shape[-1]
    rank = gla_w_gk_up.shape[1]
    bf = lambda w: w.astype(BF16)

    h = x
    for i in range(depth):
        j = i // 2
        if i % 2 == 0:
            q, k, v, lf, g = _hg_proj(h, norm_mixer_w[i], lb_logits, bf(hg_w_in[j]), i, hg_heads)
            hnw, w_out = hg_norm_w[j], hg_w_out[j]
        else:
            n_main = 2 * gla_kd + 2 * gla_vd
            w_in = gla_w_in[j]
            w_lr = jnp.pad(w_in[:, n_main:], ((0, 0), (0, LANES - rank)))
            w_up = jnp.pad(gla_w_gk_up[j], ((0, LANES - rank), (0, 0)))
            q, k, v, lf, g = _gla_proj(h, norm_mixer_w[i], bf(w_in[:, :n_main]), bf(w_lr), bf(w_up),
                                       gla_b_gk_up[j], gla_heads, gla_kd, gla_vd)
            hnw, w_out = gla_norm_w[j], gla_w_out[j]
        o = _attention(q, k, v, lf)
        h = _dense(o, g, hnw, bf(w_out), h, norm_ffn_w[i], bf(ffn_w_up[i]), ffn_conv_w[i], ffn_conv_b[i],
                   bf(ffn_w_down[i]), norm_final_w, i == depth - 1)
    return h
```
